```python
import math
import jax
import jax.numpy as jnp
from jax import lax
import numpy as np

D_MODEL = 2048
BATCH = 4
SEQ = 4096
DEPTH = 4

CTX_LEN = 256
GRID_W = 64
D_MIX = D_MODEL
ATTN_W = D_MIX // 2
SSM_W = D_MIX // 4
GMLP_W = D_MIX - ATTN_W - SSM_W
HEAD_DIM = 64
N_HEADS = ATTN_W // HEAD_DIM
GQA_RATIO = 8
N_KV_HEADS = N_HEADS // GQA_RATIO
KV_W = N_KV_HEADS * HEAD_DIM
WINDOW = 128
ATTN_BLOCK = 128
ROPE_BASE = 10000.0
SSM_GROUP = 16
SSM_GROUPS = SSM_W // SSM_GROUP
SSM_STATE = 64
DT_MIN = 0.001
DT_MAX = 0.1
GMLP_CHUNK = 128
GMLP_GROUP_W = 128
GMLP_GROUPS = GMLP_W // GMLP_GROUP_W
D_FF = ((8 * D_MODEL // 3 + 255) // 256) * 256
CONV_W = 3
N_MOD = 6
NORM_EPS = 1e-6
OFF_K = ATTN_W
OFF_V = OFF_K + KV_W
OFF_S = OFF_V + KV_W
OFF_GU = OFF_S + SSM_W
OFF_GV = OFF_GU + GMLP_W
N_IN = OFF_GV + GMLP_W

kernel_name = 'hymba_style_s5_swa_gmlp_convffn_dit'


def rms_norm(x, g):
    xf = x.astype(jnp.float32)
    y = xf * lax.rsqrt(jnp.mean(xf * xf, axis=-1, keepdims=True) + NORM_EPS)
    return (y * g.astype(jnp.float32)).astype(x.dtype)


def layer_norm(x, g, b):
    xf = x.astype(jnp.float32)
    mu = jnp.mean(xf, axis=-1, keepdims=True)
    var = jnp.mean(jnp.square(xf - mu), axis=-1, keepdims=True)
    return ((xf - mu) * lax.rsqrt(var + NORM_EPS) * g.astype(jnp.float32) + b.astype(jnp.float32)).astype(x.dtype)


def modulate(h, shift, scale):
    return h * (1 + scale) + shift


def axial_rope_angles(rows):
    row = jnp.repeat(jnp.arange(rows), GRID_W)
    col = jnp.tile(jnp.arange(GRID_W), rows)
    n_freq = HEAD_DIM // 4
    inv_freq = ROPE_BASE ** (-jnp.arange(n_freq, dtype=jnp.float32) / n_freq)
    ang = jnp.stack([row, col], axis=-1).astype(jnp.float32)[:, :, None] * inv_freq
    return jnp.cos(ang), jnp.sin(ang)


def apply_rope(x, cos, sin):
    b, l, h, _ = x.shape
    xr = x.astype(jnp.float32).reshape(b, l, h, 2, 2, HEAD_DIM // 4)
    x1, x2 = xr[..., 0, :], xr[..., 1, :]
    cs, sn = cos[None, :, None], sin[None, :, None]
    out = jnp.stack([x1 * cs - x2 * sn, x1 * sn + x2 * cs], axis=-2)
    return out.reshape(b, l, h, HEAD_DIM).astype(x.dtype)


def window_attention(q, k, v, kc, vc, sink):
    b, l, _, _ = q.shape
    nb = l // ATTN_BLOCK
    n_c = kc.shape[1]
    scale = HEAD_DIM ** -0.5
    qb = q.reshape(b, nb, ATTN_BLOCK, N_KV_HEADS, GQA_RATIO, HEAD_DIM)
    pad = ((0, 0), (ATTN_BLOCK, ATTN_BLOCK), (0, 0), (0, 0))

    def band(t):
        tb = jnp.pad(t, pad).reshape(b, nb + 2, ATTN_BLOCK, N_KV_HEADS, HEAD_DIM)
        return jnp.concatenate([tb[:, :-2], tb[:, 1:-1], tb[:, 2:]], axis=2)

    kw, vw = band(k), band(v)
    s_win = jnp.einsum('bnqkgd,bnjkd->bnkgqj', qb, kw).astype(jnp.float32) * scale
    blk = jnp.arange(nb)[:, None, None]
    qpos = blk * ATTN_BLOCK + jnp.arange(ATTN_BLOCK)[None, :, None]
    kpos = (blk - 1) * ATTN_BLOCK + jnp.arange(3 * ATTN_BLOCK)[None, None, :]
    mask = (jnp.abs(kpos - qpos) <= WINDOW) & (kpos >= 0) & (kpos < l)
    s_win = jnp.where(mask[None, :, None, None], s_win, -jnp.inf)
    s_ctx = jnp.einsum('bnqkgd,bckd->bnkgqc', qb, kc).astype(jnp.float32) * scale
    s_sink = jnp.broadcast_to(sink.astype(jnp.float32).reshape(1, 1, N_KV_HEADS, GQA_RATIO, 1, 1),
                              s_win.shape[:-1] + (1,))
    p = jax.nn.softmax(jnp.concatenate([s_win, s_ctx, s_sink], axis=-1), axis=-1).astype(v.dtype)
    nw = 3 * ATTN_BLOCK
    o = (jnp.einsum('bnkgqj,bnjkd->bnqkgd', p[..., :nw], vw)
         + jnp.einsum('bnkgqc,bckd->bnqkgd', p[..., nw:nw + n_c], vc))
    return o.reshape(b, l, N_HEADS * HEAD_DIM)


def context_attention(qc, kc, vc, sink):
    b, n, _, _ = qc.shape
    qg = qc.reshape(b, n, N_KV_HEADS, GQA_RATIO, HEAD_DIM)
    s = jnp.einsum('bqkgd,bjkd->bkgqj', qg, kc).astype(jnp.float32) * HEAD_DIM ** -0.5
    s_sink = jnp.broadcast_to(sink.astype(jnp.float32).reshape(1, N_KV_HEADS, GQA_RATIO, 1, 1), s.shape[:-1] + (1,))
    p = jax.nn.softmax(jnp.concatenate([s, s_sink], axis=-1), axis=-1).astype(vc.dtype)
    o = jnp.einsum('bkgqj,bjkd->bqkgd', p[..., :n], vc)
    return o.reshape(b, n, N_HEADS * HEAD_DIM)


def s5_discretize(lam_re, lam_im, log_dt, b_re, b_im):
    lam_re = lam_re.astype(jnp.float32)
    lam_im = lam_im.astype(jnp.float32)
    dt = jnp.exp(log_dt.astype(jnp.float32))[:, None]
    mag = jnp.exp(lam_re * dt)
    a_re = mag * jnp.cos(lam_im * dt)
    a_im = mag * jnp.sin(lam_im * dt)
    den = lam_re * lam_re + lam_im * lam_im
    n_re = a_re - 1.0
    f_re = (n_re * lam_re + a_im * lam_im) / den
    f_im = (a_im * lam_re - n_re * lam_im) / den
    b_re = b_re.astype(jnp.float32)
    b_im = b_im.astype(jnp.float32)
    bb_re = f_re[..., None] * b_re - f_im[..., None] * b_im
    bb_im = f_re[..., None] * b_im + f_im[..., None] * b_re
    return a_re, a_im, bb_re, bb_im


def complex_affine_combine(e1, e2):
    a1r, a1i, b1r, b1i = e1
    a2r, a2i, b2r, b2i = e2
    return (a2r * a1r - a2i * a1i, a2r * a1i + a2i * a1r,
            a2r * b1r - a2i * b1i + b2r, a2r * b1i + a2i * b1r + b2i)


def s5_states(u, disc, h0, reverse):
    a_re, a_im, bb_re, bb_im = disc
    bu_re = jnp.einsum('gph,bngh->bngp', bb_re, u)
    bu_im = jnp.einsum('gph,bngh->bngp', bb_im, u)
    if h0 is not None:
        h_re, h_im = h0
        first = -1 if reverse else 0
        bu_re = bu_re.at[:, first].add(a_re * h_re - a_im * h_im)
        bu_im = bu_im.at[:, first].add(a_re * h_im + a_im * h_re)
    shape = bu_re.shape
    elems = (jnp.broadcast_to(a_re, shape), jnp.broadcast_to(a_im, shape), bu_re, bu_im)
    _, _, s_re, s_im = lax.associative_scan(complex_affine_combine, elems, reverse=reverse, axis=1)
    return s_re, s_im


def s5_readout(s_re, s_im, c_re, c_im):
    return jnp.einsum('ghp,bngp->bngh', c_re, s_re) - jnp.einsum('ghp,bngp->bngh', c_im, s_im)


def s5_glu(y, w_glu, b_glu):
    g = jax.nn.gelu(y)
    return g * jax.nn.sigmoid(g @ w_glu + b_glu)


def s5_mixer(u, uc, lam_re, lam_im, log_dt, b_re, b_im, c_re, c_im, d_skip, w_glu, b_glu, ctx_out):
    dtype = u.dtype
    bsz, n, _ = u.shape
    n_c = uc.shape[1]
    u4 = u.astype(jnp.float32).reshape(bsz, n, SSM_GROUPS, SSM_GROUP)
    uc4 = uc.astype(jnp.float32).reshape(bsz, n_c, SSM_GROUPS, SSM_GROUP)
    d4 = d_skip.astype(jnp.float32).reshape(SSM_GROUPS, SSM_GROUP)
    y = d4 * u4
    yc = d4 * uc4 if ctx_out else None
    for direction, rev in enumerate((False, True)):
        disc = s5_discretize(lam_re[direction], lam_im[direction], log_dt[direction], b_re[direction], b_im[direction])
        cr = c_re[direction].astype(jnp.float32)
        ci = c_im[direction].astype(jnp.float32)
        sc_re, sc_im = s5_states(uc4, disc, None, rev)
        end = 0 if rev else -1
        s_re, s_im = s5_states(u4, disc, (sc_re[:, end], sc_im[:, end]), rev)
        y = y + s5_readout(s_re, s_im, cr, ci)
        if ctx_out:
            yc = yc + s5_readout(sc_re, sc_im, cr, ci)
    out = s5_glu(y.reshape(bsz, n, SSM_W).astype(dtype), w_glu, b_glu)
    out_c = s5_glu(yc.reshape(bsz, n_c, SSM_W).astype(dtype), w_glu, b_glu) if ctx_out else None
    return out, out_c


def gmlp_spatial_gate(gu, gv, ln_g, ln_b, w_s, b_s):
    bsz, n, _ = gu.shape
    u = jax.nn.gelu(gu)
    v = layer_norm(jax.nn.gelu(gv), ln_g, ln_b)
    vch = v.reshape(bsz, n // GMLP_CHUNK, GMLP_CHUNK, GMLP_GROUPS, GMLP_GROUP_W)
    mixed = jnp.einsum('gij,bnjgc->bnigc', w_s, vch) + b_s.T[None, None, :, :, None]
    return u * mixed.reshape(bsz, n, GMLP_W)


def conv_ffn(h, w_up, conv_w, conv_b, w_down):
    n = h.shape[1]
    up = h @ w_up
    gate, val = up[..., :D_FF], up[..., D_FF:]
    half = CONV_W // 2
    gp = jnp.pad(gate, ((0, 0), (half, half), (0, 0)))
    gate = sum(gp[:, j:j + n] * conv_w[j] for j in range(CONV_W)) + conv_b
    return (jax.nn.silu(gate) * val) @ w_down


def setup_inputs(seed: int = 0) -> dict:
    key = jax.random.key(seed)
    ks = iter(jax.random.split(key, 40))

    def nrm(shape, std):
        return std * jax.random.normal(next(ks), shape, jnp.float32)

    x = nrm((BATCH, SEQ, D_MODEL), 1.0)
    c = nrm((BATCH, D_MODEL), 1.0)
    ctx = nrm((BATCH, CTX_LEN, D_MODEL), 1.0)
    c_ctx = nrm((D_MODEL,), 1.0)
    w_ada = nrm((DEPTH, D_MODEL, N_MOD * D_MODEL), 0.5 * D_MODEL ** -0.5)
    b_ada = nrm((DEPTH, N_MOD * D_MODEL), 0.02)
    g_mix = 1.0 + nrm((DEPTH, D_MODEL), 0.02)
    g_ffn = 1.0 + nrm((DEPTH, D_MODEL), 0.02)
    w_in = nrm((DEPTH, D_MODEL, N_IN), D_MODEL ** -0.5)
    w_out = nrm((DEPTH, D_MIX, D_MODEL), D_MIX ** -0.5)
    attn_sink = nrm((DEPTH, N_HEADS), 0.5)
    ssm_shape = (DEPTH, 2, SSM_GROUPS, SSM_STATE)
    ssm_lambda_re = -0.5 + nrm(ssm_shape, 0.01)
    ssm_lambda_im = math.pi * jnp.arange(SSM_STATE, dtype=jnp.float32) + nrm(ssm_shape, 0.01)
    ssm_log_dt = jax.random.uniform(next(ks), (DEPTH, 2, SSM_GROUPS), jnp.float32,
                                    minval=math.log(DT_MIN), maxval=math.log(DT_MAX))
    ssm_b_re = nrm((DEPTH, 2, SSM_GROUPS, SSM_STATE, SSM_GROUP), SSM_GROUP ** -0.5)
    ssm_b_im = nrm((DEPTH, 2, SSM_GROUPS, SSM_STATE, SSM_GROUP), SSM_GROUP ** -0.5)
    ssm_c_re = nrm((DEPTH, 2, SSM_GROUPS, SSM_GROUP, SSM_STATE), SSM_STATE ** -0.5)
    ssm_c_im = nrm((DEPTH, 2, SSM_GROUPS, SSM_GROUP, SSM_STATE), SSM_STATE ** -0.5)
    ssm_d = nrm((DEPTH, SSM_W), 1.0)
    ssm_w_glu = nrm((DEPTH, SSM_W, SSM_W), SSM_W ** -0.5)
    ssm_b_glu = nrm((DEPTH, SSM_W), 0.02)
    gmlp_ln_g = 1.0 + nrm((DEPTH, GMLP_W), 0.02)
    gmlp_ln_b = nrm((DEPTH, GMLP_W), 0.02)
    gmlp_w_s = nrm((DEPTH, GMLP_GROUPS, GMLP_CHUNK, GMLP_CHUNK), 0.5 * GMLP_CHUNK ** -0.5)
    gmlp_b_s = 1.0 + nrm((DEPTH, GMLP_GROUPS, GMLP_CHUNK), 0.02)
    ffn_w_up = nrm((DEPTH, D_MODEL, 2 * D_FF), D_MODEL ** -0.5)
    ffn_conv_w = nrm((DEPTH, CONV_W, D_FF), CONV_W ** -0.5)
    ffn_conv_b = nrm((DEPTH, D_FF), 0.02)
    ffn_w_down = nrm((DEPTH, D_FF, D_MODEL), D_FF ** -0.5)
    g_final = 1.0 + nrm((D_MODEL,), 0.02)
    return {'x': x, 'c': c, 'ctx': ctx, 'c_ctx': c_ctx, 'w_ada': w_ada, 'b_ada': b_ada,
            'g_mix': g_mix, 'g_ffn': g_ffn, 'w_in': w_in, 'w_out': w_out, 'attn_sink': attn_sink,
            'ssm_lambda_re': ssm_lambda_re, 'ssm_lambda_im': ssm_lambda_im, 'ssm_log_dt': ssm_log_dt,
            'ssm_b_re': ssm_b_re, 'ssm_b_im': ssm_b_im, 'ssm_c_re': ssm_c_re, 'ssm_c_im': ssm_c_im,
            'ssm_d': ssm_d, 'ssm_w_glu': ssm_w_glu, 'ssm_b_glu': ssm_b_glu,
            'gmlp_ln_g': gmlp_ln_g, 'gmlp_ln_b': gmlp_ln_b, 'gmlp_w_s': gmlp_w_s, 'gmlp_b_s': gmlp_b_s,
            'ffn_w_up': ffn_w_up, 'ffn_conv_w': ffn_conv_w, 'ffn_conv_b': ffn_conv_b, 'ffn_w_down': ffn_w_down,
            'g_final': g_final}


def reference(x, c, ctx, c_ctx, w_ada, b_ada, g_mix, g_ffn, w_in, w_out, attn_sink,
              ssm_lambda_re, ssm_lambda_im, ssm_log_dt, ssm_b_re, ssm_b_im, ssm_c_re, ssm_c_im,
              ssm_d, ssm_w_glu, ssm_b_glu, gmlp_ln_g, gmlp_ln_b, gmlp_w_s, gmlp_b_s,
              ffn_w_up, ffn_conv_w, ffn_conv_b, ffn_w_down, g_final):
    b, l, _ = x.shape
    n_c = ctx.shape[1]
    rows = l // GRID_W
    cos, sin = axial_rope_angles(rows)
    xc = ctx
    act_c = jax.nn.silu(c)
    act_cc = jax.nn.silu(c_ctx)
    for i in range(DEPTH):
        last = i == DEPTH - 1
        mod = (act_c @ w_ada[i] + b_ada[i])[:, None, :]
        mod_c = (act_cc @ w_ada[i] + b_ada[i])[None, None, :]
        shift_a, scale_a, gate_a, shift_f, scale_f, gate_f = jnp.split(mod, N_MOD, axis=-1)
        shift_ac, scale_ac, gate_ac, shift_fc, scale_fc, gate_fc = jnp.split(mod_c, N_MOD, axis=-1)

        h = modulate(rms_norm(x, g_mix[i]), shift_a, scale_a)
        hc = modulate(rms_norm(xc, g_mix[i]), shift_ac, scale_ac)
        z = h @ w_in[i]
        base = OFF_K if last else 0
        zc = hc @ (w_in[i][:, OFF_K:OFF_GU] if last else w_in[i])

        q = apply_rope(z[..., :OFF_K].reshape(b, l, N_HEADS, HEAD_DIM), cos, sin)
        k = apply_rope(z[..., OFF_K:OFF_V].reshape(b, l, N_KV_HEADS, HEAD_DIM), cos, sin)
        v = z[..., OFF_V:OFF_S].reshape(b, l, N_KV_HEADS, HEAD_DIM)
        kc = zc[..., OFF_K - base:OFF_V - base].reshape(b, n_c, N_KV_HEADS, HEAD_DIM)
        vc = zc[..., OFF_V - base:OFF_S - base].reshape(b, n_c, N_KV_HEADS, HEAD_DIM)
        o_attn = window_attention(q, k, v, kc, vc, attn_sink[i])

        o_ssm, o_ssm_c = s5_mixer(z[..., OFF_S:OFF_GU], zc[..., OFF_S - base:OFF_GU - base],
                                  ssm_lambda_re[i], ssm_lambda_im[i], ssm_log_dt[i], ssm_b_re[i], ssm_b_im[i],
                                  ssm_c_re[i], ssm_c_im[i], ssm_d[i], ssm_w_glu[i], ssm_b_glu[i], not last)

        o_gmlp = gmlp_spatial_gate(z[..., OFF_GU:OFF_GV], z[..., OFF_GV:], gmlp_ln_g[i], gmlp_ln_b[i],
                                   gmlp_w_s[i], gmlp_b_s[i])

        x = x + gate_a * (jnp.concatenate([o_attn, o_ssm, o_gmlp], axis=-1) @ w_out[i])
        x = x + gate_f * conv_ffn(modulate(rms_norm(x, g_ffn[i]), shift_f, scale_f),
                                  ffn_w_up[i], ffn_conv_w[i], ffn_conv_b[i], ffn_w_down[i])

        if not last:
            qc = zc[..., :OFF_K].reshape(b, n_c, N_HEADS, HEAD_DIM)
            o_attn_c = context_attention(qc, kc, vc, attn_sink[i])
            o_gmlp_c = gmlp_spatial_gate(zc[..., OFF_GU:OFF_GV], zc[..., OFF_GV:], gmlp_ln_g[i], gmlp_ln_b[i],
                                         gmlp_w_s[i], gmlp_b_s[i])
            xc = xc + gate_ac * (jnp.concatenate([o_attn_c, o_ssm_c, o_gmlp_c], axis=-1) @ w_out[i])
            xc = xc + gate_fc * conv_ffn(modulate(rms_norm(xc, g_ffn[i]), shift_fc, scale_fc),
                                         ffn_w_up[i], ffn_conv_w[i], ffn_conv_b[i], ffn_w_down[i])
    return rms_norm(x, g_final)
```

```python
import functools
import math

import numpy as np
import jax
import jax.numpy as jnp
from jax import lax
from jax.experimental import pallas as pl
from jax.experimental.pallas import tpu as pltpu

F32 = jnp.float32
BF16 = jnp.bfloat16

HEAD_DIM = 64
GQA_RATIO = 8
GRID_W = 64
ROPE_BASE = 10000.0
ATTN_BLOCK = 128
SSM_GROUP = 16
SSM_STATE = 64
GMLP_CHUNK = 128
GMLP_GROUP_W = 128
N_MOD = 6
NORM_EPS = 1e-6
MASK_NEG = -1e30

LANES = 128
SUBLANES = 8
BF16_ROWS = 16

MOD_TN = 1024
FFN_TM = 512
FFN_TF = 512
S5_T = 128
S5_PITCH = S5_T + 4
S5_SHIFT = 4
VMEM_LIMIT = 56 * 1024 * 1024


def _cparams(sem):
    return pltpu.CompilerParams(dimension_semantics=sem, vmem_limit_bytes=VMEM_LIMIT)


def _sigmoid(x):
    return 1.0 / (1.0 + jnp.exp(-x))


def _gelu(x):
    return 0.5 * x * (1.0 + jnp.tanh(0.7978845608028654 * (x + 0.044715 * (x * x * x))))


def _norm_mod(x, g, shift, scale):
    y = x * lax.rsqrt(jnp.mean(x * x, axis=-1, keepdims=True) + NORM_EPS)
    return (y * g) * (1.0 + scale) + shift


def _dot(a, b):
    return jnp.dot(a, b, preferred_element_type=F32)


def _resident(block_shape, index_map):
    return pl.BlockSpec(block_shape, index_map, pipeline_mode=pl.Buffered(1))


def _mod_kernel(c_ref, w_ref, b_ref, o_ref):
    c = c_ref[...]
    act = (c * _sigmoid(c)).astype(BF16)
    o_ref[...] = _dot(act, w_ref[...].astype(BF16)) + b_ref[...]


def _modulation(cpad, w_ada, b_ada):
    depth, d, n = w_ada.shape
    rows = cpad.shape[0]
    return pl.pallas_call(
        _mod_kernel,
        grid=(depth, n // MOD_TN),
        in_specs=[
            pl.BlockSpec((rows, d), lambda l, j: (0, 0)),
            pl.BlockSpec((None, d, MOD_TN), lambda l, j: (l, 0, j)),
            pl.BlockSpec((None, 1, MOD_TN), lambda l, j: (l, 0, j)),
        ],
        out_specs=pl.BlockSpec((None, rows, MOD_TN), lambda l, j: (l, 0, j)),
        out_shape=jax.ShapeDtypeStruct((depth, rows, n), F32),
        compiler_params=_cparams(("arbitrary", "arbitrary")),
        name="adaln_mod",
    )(cpad, w_ada, b_ada.reshape(depth, 1, n))


def _rope(x, c, s, lane):
    first = (lane % 32) < 16
    partner = jnp.where(first, pltpu.roll(x, LANES - 16, axis=1), pltpu.roll(x, 16, axis=1))
    return x * c + partner * s


def _in_proj_kernel(x_ref, g_ref, sh_ref, sc_ref, w_ref, cos_ref, sin_ref,
                    q_ref, kx_ref, vx_ref, su_ref, gm_ref, *, attn_w, kv_w, ssm_w):
    h = _norm_mod(x_ref[...], g_ref[...], sh_ref[...], sc_ref[...]).astype(BF16)
    z = _dot(h, w_ref[...])
    tm = z.shape[0]
    c = cos_ref[...]
    s = sin_ref[...]
    lane = lax.broadcasted_iota(jnp.int32, (tm, LANES), 1)
    qscale = HEAD_DIM ** -0.5
    for j in range(attn_w // LANES):
        zq = z[:, j * LANES:(j + 1) * LANES]
        q_ref[:, j * LANES:(j + 1) * LANES] = (_rope(zq, c, s, lane) * qscale).astype(BF16)
    low = lane < HEAD_DIM
    off_k, off_v = attn_w, attn_w + kv_w
    for src, dst, rot in ((z[:, off_k:off_k + kv_w], kx_ref, True), (z[:, off_v:off_v + kv_w], vx_ref, False)):
        t = _rope(src, c, s, lane) if rot else src
        h0 = jnp.where(low, t, 0.0)
        h1 = jnp.where(low, 0.0, t)
        parts = (h0, pltpu.roll(h0, HEAD_DIM, axis=1), pltpu.roll(h1, HEAD_DIM, axis=1), h1)
        for p, val in enumerate(parts):
            dst[:, p * LANES:(p + 1) * LANES] = val.astype(BF16)
    off_s = attn_w + 2 * kv_w
    su_ref[...] = z[:, off_s:off_s + ssm_w]
    gm_ref[...] = z[:, off_s + ssm_w:]


def _in_proj(layer, xall, g_mix, mod4, w_in, cos_t, sin_t, dims):
    B, n_c, L, ntok = dims["B"], dims["n_c"], dims["L"], dims["ntok"]
    d = xall.shape[1]
    n_in = w_in.shape[2]
    attn_w, kv_w, ssm_w = dims["attn_w"], dims["kv_w"], dims["ssm_w"]
    gm_w = n_in - attn_w - 2 * kv_w - ssm_w
    tm = n_c
    lt = L // tm
    S = n_c + L

    def row(i):
        return jnp.where(i < B, B, (i - B) // lt)

    def tbl(i):
        return jnp.where(i < B, 0, 1 + (i - B) % lt)

    def seq(i):
        return (jnp.where(i < B, i, (i - B) // lt), jnp.where(i < B, 0, 1 + (i - B) % lt), 0)

    kern = functools.partial(_in_proj_kernel, attn_w=attn_w, kv_w=kv_w, ssm_w=ssm_w)
    return pl.pallas_call(
        kern,
        grid=(ntok // tm,),
        in_specs=[
            pl.BlockSpec((tm, d), lambda i: (i, 0)),
            pl.BlockSpec((None, 1, d), lambda i: (layer, 0, 0)),
            pl.BlockSpec((None, None, 1, d), lambda i: (layer, row(i), 0, 0)),
            pl.BlockSpec((None, None, 1, d), lambda i: (layer, row(i), 0, 1)),
            _resident((None, d, n_in), lambda i: (layer, 0, 0)),
            pl.BlockSpec((tm, LANES), lambda i: (tbl(i), 0)),
            pl.BlockSpec((tm, LANES), lambda i: (tbl(i), 0)),
        ],
        out_specs=[
            pl.BlockSpec((tm, attn_w), lambda i: (i, 0)),
            pl.BlockSpec((tm, 4 * LANES), lambda i: (i, 0)),
            pl.BlockSpec((tm, 4 * LANES), lambda i: (i, 0)),
            pl.BlockSpec((None, tm, ssm_w), seq),
            pl.BlockSpec((tm, gm_w), lambda i: (i, 0)),
        ],
        out_shape=[
            jax.ShapeDtypeStruct((ntok, attn_w), BF16),
            jax.ShapeDtypeStruct((ntok, 4 * LANES), BF16),
            jax.ShapeDtypeStruct((ntok, 4 * LANES), BF16),
            jax.ShapeDtypeStruct((B, S, ssm_w), F32),
            jax.ShapeDtypeStruct((ntok, gm_w), F32),
        ],
        compiler_params=_cparams(("arbitrary",)),
        name="in_proj",
    )(xall, g_mix, mod4, mod4, w_in, cos_t, sin_t)


def _attn_kernel(sink_ref, q_ref, kp_ref, ko_ref, kn_ref, kc_ref, vp_ref, vo_ref, vn_ref, vc_ref,
                 bias_ref, o_ref, *, layer, n_kv):
    blk = q_ref.shape[0]
    per_kv = GQA_RATIO * HEAD_DIM // LANES
    bias = bias_ref[...]
    nkeys = bias.shape[1]
    lane = lax.broadcasted_iota(jnp.int32, (per_kv * blk, LANES), 1)
    for g in range(n_kv):
        def keys(refs, half):
            lo = (2 * g + half) * LANES
            return [r[:, lo:lo + LANES] for r in refs]
        krefs = (kp_ref, ko_ref, kn_ref, kc_ref)
        vrefs = (vp_ref, vo_ref, vn_ref, vc_ref)
        kb = jnp.concatenate(keys(krefs, 0) + keys(krefs, 1), axis=0)
        vb = jnp.concatenate(keys(vrefs, 0) + keys(vrefs, 1), axis=0)
        qg = jnp.concatenate([q_ref[:, (g * per_kv + c) * LANES:(g * per_kv + c + 1) * LANES]
                              for c in range(per_kv)], axis=0)
        s = lax.dot_general(qg, kb, (((1,), (1,)), ((), ())), preferred_element_type=F32)
        probs = []
        invs = []
        for p in range(2):
            sp = s[:, p * nkeys:(p + 1) * nkeys] + bias
            sink = jnp.concatenate(
                [jnp.full((blk, 1), sink_ref[layer, g * GQA_RATIO + 2 * c + p], F32) for c in range(per_kv)],
                axis=0)
            m = jnp.maximum(jnp.max(sp, axis=1, keepdims=True), sink)
            e = jnp.exp(sp - m)
            den = jnp.sum(e, axis=1, keepdims=True) + jnp.exp(sink - m)
            probs.append(e.astype(BF16))
            invs.append(1.0 / den)
        o = _dot(jnp.concatenate(probs, axis=1), vb)
        o = o * jnp.where(lane < HEAD_DIM, invs[0], invs[1])
        for c in range(per_kv):
            o_ref[:, (g * per_kv + c) * LANES:(g * per_kv + c + 1) * LANES] = (
                o[c * blk:(c + 1) * blk].astype(o_ref.dtype))


def _attn_bias(per_kv, n_c):
    blk = ATTN_BLOCK
    r = np.arange(blk)[:, None]
    j = np.arange(blk)[None, :]
    zero = np.zeros((blk, blk), np.float32)
    neg = np.full((blk, blk), MASK_NEG, np.float32)
    prev = np.where(j >= r, 0.0, MASK_NEG).astype(np.float32)
    nxt = np.where(j <= r, 0.0, MASK_NEG).astype(np.float32)
    ctx = np.zeros((blk, n_c), np.float32)
    variants = [
        np.concatenate([neg, zero, nxt, ctx], 1),
        np.concatenate([prev, zero, nxt, ctx], 1),
        np.concatenate([prev, zero, neg, ctx], 1),
        np.concatenate([neg, neg, neg, ctx], 1),
    ]
    return jnp.asarray(np.tile(np.stack(variants), (1, per_kv, 1)))


def _attention(layer, q, kx, vx, sink, bias, dims):
    B, n_c, L, ntok = dims["B"], dims["n_c"], dims["L"], dims["ntok"]
    blk = ATTN_BLOCK
    attn_w = q.shape[1]
    n_kv = dims["kv_w"] // HEAD_DIM
    nblk = ntok // blk
    nctb = B * n_c // blk
    cpb = n_c // blk
    nb = L // blk
    assert nb >= 2

    def bat(t):
        return jnp.where(t < nctb, t // cpb, (t - nctb) // nb)

    def variant(t):
        n = (t - nctb) % nb
        return jnp.where(t < nctb, 3, jnp.where(n == 0, 0, jnp.where(n == nb - 1, 2, 1)))

    kvw = kx.shape[1]
    own = pl.BlockSpec((blk, kvw), lambda t: (t, 0))
    prev = pl.BlockSpec((blk, kvw), lambda t: (jnp.maximum(t - 1, 0), 0))
    nxt = pl.BlockSpec((blk, kvw), lambda t: (jnp.minimum(t + 1, nblk - 1), 0))
    ctx = pl.BlockSpec((n_c, kvw), lambda t: (bat(t), 0))
    kern = functools.partial(_attn_kernel, layer=layer, n_kv=n_kv)
    return pl.pallas_call(
        kern,
        grid=(nblk,),
        in_specs=[
            pl.BlockSpec(memory_space=pltpu.SMEM),
            pl.BlockSpec((blk, attn_w), lambda t: (t, 0)),
            prev, own, nxt, ctx, prev, own, nxt, ctx,
            pl.BlockSpec((None,) + bias.shape[1:], lambda t: (variant(t), 0, 0)),
        ],
        out_specs=pl.BlockSpec((blk, attn_w), lambda t: (t, 0)),
        out_shape=jax.ShapeDtypeStruct((ntok, attn_w), BF16),
        compiler_params=_cparams(("arbitrary",)),
        name="window_attn",
    )(sink, q, kx, kx, kx, kx, vx, vx, vx, vx, bias)


def _s5_prep_kernel(lre_ref, lim_ref, ldt_ref, bre_ref, bim_ref, are_ref, aim_ref, bbre_ref, bbim_ref):
    lre = lre_ref[...]
    lim = lim_ref[...]
    dt = jnp.exp(ldt_ref[...])
    mag = jnp.exp(lre * dt)
    a_re = mag * jnp.cos(lim * dt)
    a_im = mag * jnp.sin(lim * dt)
    den = lre * lre + lim * lim
    n_re = a_re - 1.0
    f_re = ((n_re * lre + a_im * lim) / den)[:, None, :]
    f_im = ((a_im * lre - n_re * lim) / den)[:, None, :]
    are_ref[...] = a_re
    aim_ref[...] = a_im
    bre = bre_ref[...]
    bim = bim_ref[...]
    bbre_ref[...] = f_re * bre - f_im * bim
    bbim_ref[...] = f_re * bim + f_im * bre


def _s5_prepare(lam_re, lam_im, log_dt, b_re, b_im, c_re, c_im, B):
    depth, _, G, P = lam_re.shape
    H = b_re.shape[-1]
    n = depth * 2 * G
    bT_re = jnp.swapaxes(b_re, -1, -2).reshape(n, H, P)
    bT_im = jnp.swapaxes(b_im, -1, -2).reshape(n, H, P)
    a_re, a_im, bb_re, bb_im = pl.pallas_call(
        _s5_prep_kernel,
        out_shape=[jax.ShapeDtypeStruct((n, P), F32), jax.ShapeDtypeStruct((n, P), F32),
                   jax.ShapeDtypeStruct((n, H, P), F32), jax.ShapeDtypeStruct((n, H, P), F32)],
        name="s5_discretize",
    )(lam_re.reshape(n, P), lam_im.reshape(n, P), log_dt.reshape(n, 1), bT_re, bT_im)
    gh = G // 2
    eye = jnp.eye(gh, dtype=F32)

    def bmat(bb):
        x = bb.reshape(depth, 2, 2, gh, H, P)
        return jnp.einsum("ldkghp,gj->ldkghjp", x, eye).reshape(depth, 2, 2, gh * H, gh * P)

    def cmat(c):
        x = c.astype(F32).reshape(depth, 2, 2, gh, H, P)
        return jnp.einsum("ldkghp,gj->ldkgpjh", x, eye).reshape(depth, 2, 2, gh * P, gh * H)

    bm = jnp.concatenate([bmat(bb_re), bmat(bb_im)], axis=-1).astype(BF16)
    cm = jnp.concatenate([cmat(c_re), -cmat(c_im)], axis=-2).astype(BF16)
    a_re = jnp.tile(a_re.reshape(depth, 2, 2, gh * P), (1, 1, B, 1))
    a_im = jnp.tile(a_im.reshape(depth, 2, 2, gh * P), (1, 1, B, 1))
    return bm, cm, a_re, a_im


def _s5_kernel(uf_ref, ub_ref, bm_ref, cm_ref, are_ref, aim_ref, yf_ref, yb_ref, bu_ref, h_ref, *, B):
    T, pitch, shift = S5_T, S5_PITCH, S5_SHIFT
    R = 2 * B
    wide = T + SUBLANES
    hw = bm_ref.shape[2]
    sw = are_ref.shape[2]
    nsl = sw // LANES
    u_refs = (uf_ref, ub_ref)
    y_refs = (yf_ref, yb_ref)

    @pl.when(pl.program_id(0) == 0)
    def _():
        h_ref[...] = jnp.zeros_like(h_ref)

    def window(b, k):
        r = 2 * b + k
        return (r * pitch - shift, wide) if k else (r * pitch, T)

    for d in range(2):
        for k in range(2):
            parts = []
            for b in range(B):
                ub = u_refs[d][b, :, k * hw:(k + 1) * hw]
                if k:
                    ub = pltpu.roll(jnp.concatenate([ub, jnp.zeros((SUBLANES, hw), F32)], axis=0), shift, axis=0)
                parts.append(ub)
            bu = _dot(jnp.concatenate(parts, axis=0).astype(BF16), bm_ref[d, k])
            rows = wide if k else T
            for b in range(B):
                base, _ = window(b, k)
                for sl in range(2 * nsl):
                    bu_ref[d, sl, base:base + rows, :] = bu[b * rows:(b + 1) * rows, sl * LANES:(sl + 1) * LANES]

    def step(d, t, h_re, h_im):
        idx = pl.ds(t, R, stride=pitch)
        b_re = jnp.concatenate([bu_ref[d, sl, idx, :] for sl in range(nsl)], axis=1)
        b_im = jnp.concatenate([bu_ref[d, nsl + sl, idx, :] for sl in range(nsl)], axis=1)
        a_re = are_ref[d]
        a_im = aim_ref[d]
        n_re = a_re * h_re - a_im * h_im + b_re
        n_im = a_re * h_im + a_im * h_re + b_im
        for sl in range(nsl):
            bu_ref[d, sl, idx, :] = n_re[:, sl * LANES:(sl + 1) * LANES]
            bu_ref[d, nsl + sl, idx, :] = n_im[:, sl * LANES:(sl + 1) * LANES]
        return n_re, n_im

    def body(j, carry):
        f_re, f_im, r_re, r_im = carry
        f_re, f_im = step(0, j, f_re, f_im)
        r_re, r_im = step(1, T - 1 - j, r_re, r_im)
        return f_re, f_im, r_re, r_im

    carry = lax.fori_loop(0, T, body, (h_ref[0, 0], h_ref[0, 1], h_ref[1, 0], h_ref[1, 1]), unroll=2)
    h_ref[0, 0] = carry[0]
    h_ref[0, 1] = carry[1]
    h_ref[1, 0] = carry[2]
    h_ref[1, 1] = carry[3]

    for d in range(2):
        for k in range(2):
            parts = []
            for b in range(B):
                base, rows = window(b, k)
                parts.append(jnp.concatenate([bu_ref[d, sl, base:base + rows, :] for sl in range(2 * nsl)], axis=1))
            y = _dot(jnp.concatenate(parts, axis=0).astype(BF16), cm_ref[d, k])
            for b in range(B):
                _, rows = window(b, k)
                yb = y[b * rows:(b + 1) * rows]
                if k:
                    yb = pltpu.roll(yb, rows - shift, axis=0)[:T]
                y_refs[d][b, :, k * hw:(k + 1) * hw] = yb


def _s5_scan(layer, su, bm, cm, a_re, a_im, dims):
    B, n_c, L = dims["B"], dims["n_c"], dims["L"]
    T = S5_T
    S = n_c + L
    ssm_w = su.shape[2]
    ncb, nlb = n_c // T, L // T
    R = 2 * B
    sw = a_re.shape[3]
    nslab = 2 * sw // LANES

    def bwd(s):
        return jnp.where(s < ncb, ncb - 1 - s, ncb + (nlb - 1 - (s - ncb)))

    blk = (B, T, ssm_w)
    kern = functools.partial(_s5_kernel, B=B)
    return pl.pallas_call(
        kern,
        grid=(S // T,),
        in_specs=[
            pl.BlockSpec(blk, lambda s: (0, s, 0)),
            pl.BlockSpec(blk, lambda s: (0, bwd(s), 0)),
            _resident((None,) + bm.shape[1:], lambda s: (layer, 0, 0, 0, 0)),
            _resident((None,) + cm.shape[1:], lambda s: (layer, 0, 0, 0, 0)),
            _resident((None,) + a_re.shape[1:], lambda s: (layer, 0, 0, 0)),
            _resident((None,) + a_im.shape[1:], lambda s: (layer, 0, 0, 0)),
        ],
        out_specs=[pl.BlockSpec(blk, lambda s: (0, s, 0)), pl.BlockSpec(blk, lambda s: (0, bwd(s), 0))],
        out_shape=[jax.ShapeDtypeStruct(su.shape, F32), jax.ShapeDtypeStruct(su.shape, F32)],
        scratch_shapes=[
            pltpu.VMEM((2, nslab, R * S5_PITCH, LANES), F32),
            pltpu.VMEM((2, 2, R, sw), F32),
        ],
        compiler_params=_cparams(("arbitrary",)),
        name="s5_scan",
    )(su, su, bm, cm, a_re, a_im)


def _mix_out_kernel(x_ref, oa_ref, yf_ref, yb_ref, su_ref, gm_ref, d_ref, wg_ref, bg_ref, lg_ref, lb_ref,
                    ws_ref, bs_ref, wo_ref, gate_ref, o_ref):
    y = d_ref[...] * su_ref[...] + yf_ref[...] + yb_ref[...]
    g1 = _gelu(y)
    o_ssm = g1 * _sigmoid(_dot(g1.astype(BF16), wg_ref[...]) + bg_ref[...])
    gw = gm_ref.shape[1] // 2
    u = _gelu(gm_ref[:, :gw])
    v = _gelu(gm_ref[:, gw:])
    mu = jnp.mean(v, axis=-1, keepdims=True)
    var = jnp.mean(jnp.square(v - mu), axis=-1, keepdims=True)
    v = ((v - mu) * lax.rsqrt(var + NORM_EPS) * lg_ref[...] + lb_ref[...]).astype(BF16)
    tm = u.shape[0]
    bs = bs_ref[...]
    rows = []
    for c in range(tm // GMLP_CHUNK):
        cols = []
        for g in range(gw // GMLP_GROUP_W):
            vb = v[c * GMLP_CHUNK:(c + 1) * GMLP_CHUNK, g * GMLP_GROUP_W:(g + 1) * GMLP_GROUP_W]
            cols.append(_dot(ws_ref[g], vb) + bs[:, g:g + 1])
        rows.append(jnp.concatenate(cols, axis=1))
    o_gmlp = u * jnp.concatenate(rows, axis=0)
    mix = jnp.concatenate([oa_ref[...], o_ssm.astype(BF16), o_gmlp.astype(BF16)], axis=1)
    o_ref[...] = x_ref[...] + gate_ref[...] * _dot(mix, wo_ref[...])


def _mix_out(layer, xall, o_attn, y_f, y_b, su, gm, p, mod4, dims):
    B, n_c, L, ntok = dims["B"], dims["n_c"], dims["L"], dims["ntok"]
    d = xall.shape[1]
    tm = n_c
    lt = L // tm
    attn_w, ssm_w, gm_w = o_attn.shape[1], su.shape[2], gm.shape[1]
    ngrp = p["w_s"].shape[1]

    def row(i):
        return jnp.where(i < B, B, (i - B) // lt)

    def seq(i):
        return (jnp.where(i < B, i, (i - B) // lt), jnp.where(i < B, 0, 1 + (i - B) % lt), 0)

    vec = lambda w: pl.BlockSpec((None, 1, w), lambda i: (layer, 0, 0))
    return pl.pallas_call(
        _mix_out_kernel,
        grid=(ntok // tm,),
        in_specs=[
            pl.BlockSpec((tm, d), lambda i: (i, 0)),
            pl.BlockSpec((tm, attn_w), lambda i: (i, 0)),
            pl.BlockSpec((None, tm, ssm_w), seq),
            pl.BlockSpec((None, tm, ssm_w), seq),
            pl.BlockSpec((None, tm, ssm_w), seq),
            pl.BlockSpec((tm, gm_w), lambda i: (i, 0)),
            vec(ssm_w),
            _resident((None, ssm_w, ssm_w), lambda i: (layer, 0, 0)),
            vec(ssm_w), vec(gm_w // 2), vec(gm_w // 2),
            _resident((None, ngrp, GMLP_CHUNK, GMLP_CHUNK), lambda i: (layer, 0, 0, 0)),
            pl.BlockSpec((None, GMLP_CHUNK, ngrp), lambda i: (layer, 0, 0)),
            _resident((None, d, d), lambda i: (layer, 0, 0)),
            pl.BlockSpec((None, None, 1, d), lambda i: (layer, row(i), 0, 2)),
        ],
        out_specs=pl.BlockSpec((tm, d), lambda i: (i, 0)),
        out_shape=jax.ShapeDtypeStruct(xall.shape, F32),
        compiler_params=_cparams(("arbitrary",)),
        name="mix_out",
    )(xall, o_attn, y_f, y_b, su, gm, p["ssm_d"], p["w_glu"], p["b_glu"], p["ln_g"], p["ln_b"],
      p["w_s"], p["b_s_t"], p["w_out"], mod4)


def _ffn_kernel(x_ref, xp_ref, xn_ref, g_ref, sh_ref, sc_ref, gate_ref, wg_ref, wv_ref, cw_ref, cb_ref, wd_ref,
                o_ref, h_ref, mp_ref, mn_ref, *, n_ctx_tiles, n_c, L):
    i = pl.program_id(0)
    j = pl.program_id(1)
    tm = x_ref.shape[0]
    halo = xp_ref.shape[0]

    @pl.when(j == 0)
    def _():
        g, sh, sc = g_ref[...], sh_ref[...], sc_ref[...]
        h_ref[0:halo, :] = _norm_mod(xp_ref[...], g, sh, sc).astype(BF16)
        h_ref[halo:halo + tm, :] = _norm_mod(x_ref[...], g, sh, sc).astype(BF16)
        h_ref[halo + tm:, :] = _norm_mod(xn_ref[...], g, sh, sc).astype(BF16)
        is_ctx = i < n_ctx_tiles
        seq_len = jnp.where(is_ctx, n_c, L)
        start = jnp.where(is_ctx, i * tm, (i - n_ctx_tiles) * tm)
        pos = (start + lax.broadcasted_iota(jnp.int32, (tm, LANES), 0)) % seq_len
        mp_ref[...] = jnp.where(pos == 0, 0.0, 1.0)
        mn_ref[...] = jnp.where(pos == seq_len - 1, 0.0, 1.0)
        o_ref[...] = jnp.zeros_like(o_ref)

    tf = wg_ref.shape[1]
    rep = tf // LANES
    ge = _dot(h_ref[...], wg_ref[...])
    val = _dot(h_ref[halo:halo + tm, :], wv_ref[...])
    ext = tm + 2 * halo
    g_prev = pltpu.roll(ge, 1, axis=0)[halo:halo + tm]
    g_next = pltpu.roll(ge, ext - 1, axis=0)[halo:halo + tm]
    mp = jnp.concatenate([mp_ref[...]] * rep, axis=1)
    mn = jnp.concatenate([mn_ref[...]] * rep, axis=1)
    cw = cw_ref[...]
    gc = cw[0:1] * (g_prev * mp) + cw[1:2] * ge[halo:halo + tm] + cw[2:3] * (g_next * mn) + cb_ref[...]
    act = (gc * _sigmoid(gc) * val).astype(BF16)
    o_ref[...] += _dot(act, wd_ref[...])

    @pl.when(j == pl.num_programs(1) - 1)
    def _():
        o_ref[...] = x_ref[...] + gate_ref[...] * o_ref[...]


def _conv_ffn(layer, xall, g_ffn, mod4, w_up, conv_w, conv_b, w_down, dims):
    B, n_c, L, ntok = dims["B"], dims["n_c"], dims["L"], dims["ntok"]
    d = xall.shape[1]
    d_ff = w_down.shape[1]
    tm, tf = dims["ffn_tm"], FFN_TF
    halo = BF16_ROWS
    nf = d_ff // tf
    nct = B * n_c // tm
    lt = L // tm
    nhalo = ntok // halo

    def row(i):
        return jnp.where(i < nct, B, (i - nct) // lt)

    modspec = lambda k: pl.BlockSpec((None, None, 1, d), lambda i, j: (layer, row(i), 0, k))
    kern = functools.partial(_ffn_kernel, n_ctx_tiles=nct, n_c=n_c, L=L)
    return pl.pallas_call(
        kern,
        grid=(ntok // tm, nf),
        in_specs=[
            pl.BlockSpec((tm, d), lambda i, j: (i, 0)),
            pl.BlockSpec((halo, d), lambda i, j: (jnp.maximum(i * (tm // halo) - 1, 0), 0)),
            pl.BlockSpec((halo, d), lambda i, j: (jnp.minimum((i + 1) * (tm // halo), nhalo - 1), 0)),
            pl.BlockSpec((None, 1, d), lambda i, j: (layer, 0, 0)),
            modspec(3), modspec(4), modspec(5),
            pl.BlockSpec((None, d, tf), lambda i, j: (layer, 0, j)),
            pl.BlockSpec((None, d, tf), lambda i, j: (layer, 0, nf + j)),
            pl.BlockSpec((None, 3, tf), lambda i, j: (layer, 0, j)),
            pl.BlockSpec((None, 1, tf), lambda i, j: (layer, 0, j)),
            pl.BlockSpec((None, tf, d), lambda i, j: (layer, j, 0)),
        ],
        out_specs=pl.BlockSpec((tm, d), lambda i, j: (i, 0)),
        out_shape=jax.ShapeDtypeStruct(xall.shape, F32),
        scratch_shapes=[
            pltpu.VMEM((tm + 2 * halo, d), BF16),
            pltpu.VMEM((tm, LANES), F32),
            pltpu.VMEM((tm, LANES), F32),
        ],
        compiler_params=_cparams(("arbitrary", "arbitrary")),
        name="conv_ffn",
    )(xall, xall, xall, g_ffn, mod4, mod4, mod4, w_up, w_up, conv_w, conv_b, w_down)


def _final_norm_kernel(x_ref, g_ref, o_ref):
    x = x_ref[...]
    o_ref[...] = x * lax.rsqrt(jnp.mean(x * x, axis=-1, keepdims=True) + NORM_EPS) * g_ref[...]


def _final_norm(xall, g_final, dims):
    B, n_c, L = dims["B"], dims["n_c"], dims["L"]
    d = xall.shape[1]
    tm = n_c
    skip = B * n_c // tm
    return pl.pallas_call(
        _final_norm_kernel,
        grid=(B * L // tm,),
        in_specs=[pl.BlockSpec((tm, d), lambda i: (i + skip, 0)), pl.BlockSpec((1, d), lambda i: (0, 0))],
        out_specs=pl.BlockSpec((tm, d), lambda i: (i, 0)),
        out_shape=jax.ShapeDtypeStruct((B * L, d), F32),
        compiler_params=_cparams(("arbitrary",)),
        name="final_norm",
    )(xall, g_final.reshape(1, d))


def _rope_tables(n_c, L):
    n_freq = HEAD_DIM // 4
    t = np.arange(L)
    inv_freq = jnp.asarray(ROPE_BASE, F32) ** (-jnp.arange(n_freq, dtype=F32) / n_freq)
    pos = jnp.asarray(np.stack([t // GRID_W, t % GRID_W], axis=-1), F32)
    ang = pos[:, :, None] * inv_freq
    cos, sin = jnp.cos(ang), jnp.sin(ang)
    c64 = jnp.concatenate([cos, cos], axis=-1).reshape(L, HEAD_DIM)
    s64 = jnp.concatenate([-sin, sin], axis=-1).reshape(L, HEAD_DIM)
    reps = LANES // HEAD_DIM
    c = jnp.concatenate([jnp.ones((n_c, LANES), F32), jnp.tile(c64, (1, reps))], axis=0)
    s = jnp.concatenate([jnp.zeros((n_c, LANES), F32), jnp.tile(s64, (1, reps))], axis=0)
    return c, s


def kernel(x, c, ctx, c_ctx, w_ada, b_ada, g_mix, g_ffn, w_in, w_out, attn_sink, ssm_lambda_re, ssm_lambda_im,
           ssm_log_dt, ssm_b_re, ssm_b_im, ssm_c_re, ssm_c_im, ssm_d, ssm_w_glu, ssm_b_glu, gmlp_ln_g, gmlp_ln_b,
           gmlp_w_s, gmlp_b_s, ffn_w_up, ffn_conv_w, ffn_conv_b, ffn_w_down, g_final):
    B, L, d = x.shape
    n_c = ctx.shape[1]
    depth = w_in.shape[0]
    ssm_w = ssm_d.shape[1]
    gmlp_w = gmlp_ln_g.shape[1]
    attn_w = w_in.shape[2] - ssm_w - 2 * gmlp_w
    attn_w = attn_w * GQA_RATIO // (GQA_RATIO + 2)
    kv_w = attn_w // GQA_RATIO
    ntok = B * (n_c + L)
    ffn_tm = min(FFN_TM, B * n_c)
    assert n_c % ATTN_BLOCK == 0 and L % n_c == 0 and L % ffn_tm == 0 and (B * n_c) % ffn_tm == 0
    assert n_c % S5_T == 0 and L % S5_T == 0 and B + 1 <= SUBLANES
    dims = dict(B=B, n_c=n_c, L=L, ntok=ntok, attn_w=attn_w, kv_w=kv_w, ssm_w=ssm_w, ffn_tm=ffn_tm)

    cpad = jnp.zeros((SUBLANES, d), F32).at[:B].set(c).at[B].set(c_ctx)
    mod = _modulation(cpad, w_ada, b_ada)
    mod4 = mod.reshape(depth, SUBLANES, 1, N_MOD * d)

    bm, cm, a_re, a_im = _s5_prepare(ssm_lambda_re, ssm_lambda_im, ssm_log_dt, ssm_b_re, ssm_b_im,
                                     ssm_c_re, ssm_c_im, B)
    cos_t, sin_t = _rope_tables(n_c, L)
    bias = _attn_bias(GQA_RATIO * HEAD_DIM // LANES, n_c)

    w_in_b = w_in.astype(BF16)
    w_up_b = ffn_w_up.astype(BF16)
    w_down_b = ffn_w_down.astype(BF16)
    mixp = dict(
        ssm_d=ssm_d.reshape(depth, 1, ssm_w), w_glu=ssm_w_glu.astype(BF16), b_glu=ssm_b_glu.reshape(depth, 1, ssm_w),
        ln_g=gmlp_ln_g.reshape(depth, 1, gmlp_w), ln_b=gmlp_ln_b.reshape(depth, 1, gmlp_w),
        w_s=gmlp_w_s.astype(BF16), b_s_t=jnp.swapaxes(gmlp_b_s, 1, 2), w_out=w_out.astype(BF16))
    g_mix3 = g_mix.reshape(depth, 1, d)
    g_ffn3 = g_ffn.reshape(depth, 1, d)
    conv_b3 = ffn_conv_b.reshape(depth, 1, -1)

    xall = jnp.concatenate([ctx.reshape(B * n_c, d), x.reshape(B * L, d)], axis=0)
    for layer in range(depth):
        q, kx, vx, su, gm = _in_proj(layer, xall, g_mix3, mod4, w_in_b, cos_t, sin_t, dims)
        o_attn = _attention(layer, q, kx, vx, attn_sink, bias, dims)
        y_f, y_b = _s5_scan(layer, su, bm, cm, a_re, a_im, dims)
        xall = _mix_out(layer, xall, o_attn, y_f, y_b, su, gm, mixp, mod4, dims)
        xall = _conv_ffn(layer, xall, g_ffn3, mod4, w_up_b, ffn_conv_w, conv_b3, w_down_b, dims)
    return _final_norm(xall, g_final, dims).reshape(B, L, d)
```

```python
import functools
import math

import numpy as np
import jax
import jax.numpy as jnp
from jax import lax
from jax.experimental import pallas as pl
from jax.experimental.pallas import tpu as pltpu

F32 = jnp.float32
BF16 = jnp.bfloat16

HEAD_DIM = 64
GQA_RATIO = 8
GRID_W = 64
ROPE_BASE = 10000.0
ATTN_BLOCK = 128
SSM_GROUP = 16
SSM_STATE = 64
GMLP_CHUNK = 128
GMLP_GROUP_W = 128
N_MOD = 6
NORM_EPS = 1e-6
MASK_NEG = -1e30
LOG2E = 1.4426950408889634

LANES = 128
SUBLANES = 8
BF16_ROWS = 16

MOD_TN = 1024
FFN_TM = 512
FFN_TF = 512
ATTN_STACK = 1
S5_T = 128
S5_PITCH = S5_T + 4
S5_SHIFT = 4
VMEM_LIMIT = 56 * 1024 * 1024


def _cparams(sem):
    return pltpu.CompilerParams(dimension_semantics=sem, vmem_limit_bytes=VMEM_LIMIT)


def _sigmoid(x):
    return 1.0 / (1.0 + jnp.exp(-x))


def _gelu(x):
    return 0.5 * x * (1.0 + jnp.tanh(0.7978845608028654 * (x + 0.044715 * (x * x * x))))


def _norm_mod(x, g, shift, scale):
    y = x * lax.rsqrt(jnp.mean(x * x, axis=-1, keepdims=True) + NORM_EPS)
    return y * (g * (1.0 + scale)) + shift


def _dot(a, b):
    return jnp.dot(a, b, preferred_element_type=F32)


def _resident(block_shape, index_map):
    return pl.BlockSpec(block_shape, index_map, pipeline_mode=pl.Buffered(1))


def _mod_kernel(c_ref, w_ref, b_ref, o_ref):
    c = c_ref[...]
    act = (c * _sigmoid(c)).astype(BF16)
    o_ref[...] = _dot(act, w_ref[...].astype(BF16)) + b_ref[...]


def _modulation(cpad, w_ada, b_ada):
    depth, d, n = w_ada.shape
    rows = cpad.shape[0]
    return pl.pallas_call(
        _mod_kernel,
        grid=(depth, n // MOD_TN),
        in_specs=[
            pl.BlockSpec((rows, d), lambda l, j: (0, 0)),
            pl.BlockSpec((None, d, MOD_TN), lambda l, j: (l, 0, j)),
            pl.BlockSpec((None, 1, MOD_TN), lambda l, j: (l, 0, j)),
        ],
        out_specs=pl.BlockSpec((None, rows, MOD_TN), lambda l, j: (l, 0, j)),
        out_shape=jax.ShapeDtypeStruct((depth, rows, n), F32),
        compiler_params=_cparams(("arbitrary", "arbitrary")),
        name="adaln_mod",
    )(cpad, w_ada, b_ada.reshape(depth, 1, n))


def _rope(x, c, s, lane):
    first = (lane % 32) < 16
    partner = jnp.where(first, pltpu.roll(x, LANES - 16, axis=1), pltpu.roll(x, 16, axis=1))
    return x * c + partner * s


def _in_proj_kernel(x_ref, g_ref, sh_ref, sc_ref, w_ref, cos_ref, sin_ref,
                    q_ref, kx_ref, vx_ref, su_ref, gm_ref, *, attn_w, kv_w, ssm_w):
    h = _norm_mod(x_ref[...], g_ref[...], sh_ref[...], sc_ref[...]).astype(BF16)
    z = _dot(h, w_ref[...])
    tm = z.shape[0]
    c = cos_ref[...]
    s = sin_ref[...]
    lane = lax.broadcasted_iota(jnp.int32, (tm, LANES), 1)
    qscale = HEAD_DIM ** -0.5 * LOG2E
    for j in range(attn_w // LANES):
        zq = z[:, j * LANES:(j + 1) * LANES]
        q_ref[:, j * LANES:(j + 1) * LANES] = (_rope(zq, c, s, lane) * qscale).astype(BF16)
    low = lane < HEAD_DIM
    off_k, off_v = attn_w, attn_w + kv_w
    for src, dst, rot in ((z[:, off_k:off_k + kv_w], kx_ref, True), (z[:, off_v:off_v + kv_w], vx_ref, False)):
        t = _rope(src, c, s, lane) if rot else src
        h0 = jnp.where(low, t, 0.0)
        h1 = jnp.where(low, 0.0, t)
        parts = (h0, pltpu.roll(h0, HEAD_DIM, axis=1), pltpu.roll(h1, HEAD_DIM, axis=1), h1)
        for p, val in enumerate(parts):
            dst[:, p * LANES:(p + 1) * LANES] = val.astype(BF16)
    off_s = attn_w + 2 * kv_w
    su_ref[...] = z[:, off_s:off_s + ssm_w]
    gm_ref[...] = z[:, off_s + ssm_w:]


def _in_proj(layer, xall, g_mix, mod4, w_in, cos_t, sin_t, dims):
    B, n_c, L, ntok = dims["B"], dims["n_c"], dims["L"], dims["ntok"]
    d = xall.shape[1]
    n_in = w_in.shape[2]
    attn_w, kv_w, ssm_w = dims["attn_w"], dims["kv_w"], dims["ssm_w"]
    gm_w = n_in - attn_w - 2 * kv_w - ssm_w
    tm = n_c
    lt = L // tm
    S = n_c + L

    def row(i):
        return jnp.where(i < B, B, (i - B) // lt)

    def tbl(i):
        return jnp.where(i < B, 0, 1 + (i - B) % lt)

    def seq(i):
        return (jnp.where(i < B, i, (i - B) // lt), jnp.where(i < B, 0, 1 + (i - B) % lt), 0)

    kern = functools.partial(_in_proj_kernel, attn_w=attn_w, kv_w=kv_w, ssm_w=ssm_w)
    return pl.pallas_call(
        kern,
        grid=(ntok // tm,),
        in_specs=[
            pl.BlockSpec((tm, d), lambda i: (i, 0)),
            pl.BlockSpec((None, 1, d), lambda i: (layer, 0, 0)),
            pl.BlockSpec((None, None, 1, d), lambda i: (layer, row(i), 0, 0)),
            pl.BlockSpec((None, None, 1, d), lambda i: (layer, row(i), 0, 1)),
            _resident((None, d, n_in), lambda i: (layer, 0, 0)),
            pl.BlockSpec((tm, LANES), lambda i: (tbl(i), 0)),
            pl.BlockSpec((tm, LANES), lambda i: (tbl(i), 0)),
        ],
        out_specs=[
            pl.BlockSpec((tm, attn_w), lambda i: (i, 0)),
            pl.BlockSpec((tm, 4 * LANES), lambda i: (i, 0)),
            pl.BlockSpec((tm, 4 * LANES), lambda i: (i, 0)),
            pl.BlockSpec((None, tm, ssm_w), seq),
            pl.BlockSpec((tm, gm_w), lambda i: (i, 0)),
        ],
        out_shape=[
            jax.ShapeDtypeStruct((ntok, attn_w), BF16),
            jax.ShapeDtypeStruct((ntok, 4 * LANES), BF16),
            jax.ShapeDtypeStruct((ntok, 4 * LANES), BF16),
            jax.ShapeDtypeStruct((B, S, ssm_w), F32),
            jax.ShapeDtypeStruct((ntok, gm_w), F32),
        ],
        compiler_params=_cparams(("arbitrary",)),
        name="in_proj",
    )(xall, g_mix, mod4, mod4, w_in, cos_t, sin_t)


def _attn_kernel(sink_ref, q_ref, kp_ref, ko_ref, kn_ref, kc_ref, vp_ref, vo_ref, vn_ref, vc_ref,
                 bias_ref, o_ref, *, layer, n_kv):
    blk = q_ref.shape[0]
    per_kv = GQA_RATIO * HEAD_DIM // LANES
    bias = jnp.concatenate([bias_ref[...]] * ATTN_STACK, axis=0)
    nkeys = bias.shape[1]
    low = lax.broadcasted_iota(jnp.int32, (ATTN_STACK * blk, LANES), 1) < HEAD_DIM
    krefs = (kp_ref, ko_ref, kn_ref, kc_ref)
    vrefs = (vp_ref, vo_ref, vn_ref, vc_ref)
    for g in range(n_kv):
        def keys(refs, half):
            lo = (2 * g + half) * LANES
            return [r[:, lo:lo + LANES] for r in refs]
        kb = jnp.concatenate(keys(krefs, 0) + keys(krefs, 1), axis=0)
        vb = jnp.concatenate(keys(vrefs, 0) + keys(vrefs, 1), axis=0)
        for c0 in range(0, per_kv, ATTN_STACK):
            cols = [(g * per_kv + c0 + c) * LANES for c in range(ATTN_STACK)]
            qs = jnp.concatenate([q_ref[:, col:col + LANES] for col in cols], axis=0)
            s = lax.dot_general(qs, kb, (((1,), (1,)), ((), ())),
                                preferred_element_type=F32)
            probs = []
            invs = []
            for p in range(2):
                sink = jnp.concatenate(
                    [jnp.full((blk, 1), sink_ref[layer, g * GQA_RATIO + 2 * (c0 + c) + p] * LOG2E, F32)
                     for c in range(ATTN_STACK)], axis=0)
                sp = s[:, p * nkeys:(p + 1) * nkeys] + bias
                m = jnp.maximum(jnp.max(sp, axis=1, keepdims=True), sink)
                e = jnp.exp2(sp - m)
                den = jnp.sum(e, axis=1, keepdims=True) + jnp.exp2(sink - m)
                probs.append(e.astype(BF16))
                invs.append(1.0 / den)
            o = _dot(jnp.concatenate(probs, axis=1), vb)
            o = o * jnp.where(low, invs[0], invs[1])
            for c, col in enumerate(cols):
                o_ref[:, col:col + LANES] = o[c * blk:(c + 1) * blk].astype(o_ref.dtype)


def _attn_bias(n_c):
    blk = ATTN_BLOCK
    r = np.arange(blk)[:, None]
    j = np.arange(blk)[None, :]
    zero = np.zeros((blk, blk), np.float32)
    neg = np.full((blk, blk), MASK_NEG, np.float32)
    prev = np.where(j >= r, 0.0, MASK_NEG).astype(np.float32)
    nxt = np.where(j <= r, 0.0, MASK_NEG).astype(np.float32)
    ctx = np.zeros((blk, n_c), np.float32)
    variants = [
        np.concatenate([neg, zero, nxt, ctx], 1),
        np.concatenate([prev, zero, nxt, ctx], 1),
        np.concatenate([prev, zero, neg, ctx], 1),
        np.concatenate([neg, neg, neg, ctx], 1),
    ]
    return jnp.asarray(np.stack(variants))


def _attention(layer, q, kx, vx, sink, bias, dims, skip_ctx):
    B, n_c, L, ntok = dims["B"], dims["n_c"], dims["L"], dims["ntok"]
    blk = ATTN_BLOCK
    attn_w = q.shape[1]
    n_kv = dims["kv_w"] // HEAD_DIM
    nblk = ntok // blk
    nctb = B * n_c // blk
    cpb = n_c // blk
    nb = L // blk
    assert nb >= 2
    off = nctb if skip_ctx else 0

    def bat(t):
        return jnp.where(t < nctb, t // cpb, (t - nctb) // nb)

    def variant(t):
        n = (t - nctb) % nb
        return jnp.where(t < nctb, 3, jnp.where(n == 0, 0, jnp.where(n == nb - 1, 2, 1)))

    kvw = kx.shape[1]
    own = pl.BlockSpec((blk, kvw), lambda t: (t + off, 0))
    prev = pl.BlockSpec((blk, kvw), lambda t: (jnp.maximum(t + off - 1, 0), 0))
    nxt = pl.BlockSpec((blk, kvw), lambda t: (jnp.minimum(t + off + 1, nblk - 1), 0))
    ctx = pl.BlockSpec((n_c, kvw), lambda t: (bat(t + off), 0))
    kern = functools.partial(_attn_kernel, layer=layer, n_kv=n_kv)
    return pl.pallas_call(
        kern,
        grid=(nblk - off,),
        in_specs=[
            pl.BlockSpec(memory_space=pltpu.SMEM),
            pl.BlockSpec((blk, attn_w), lambda t: (t + off, 0)),
            prev, own, nxt, ctx, prev, own, nxt, ctx,
            pl.BlockSpec((None,) + bias.shape[1:], lambda t: (variant(t + off), 0, 0)),
        ],
        out_specs=pl.BlockSpec((blk, attn_w), lambda t: (t, 0)),
        out_shape=jax.ShapeDtypeStruct(((nblk - off) * blk, attn_w), BF16),
        compiler_params=_cparams(("arbitrary",)),
        name="window_attn",
    )(sink, q, kx, kx, kx, kx, vx, vx, vx, vx, bias)


def _s5_prep_kernel(lre_ref, lim_ref, ldt_ref, bre_ref, bim_ref, are_ref, aim_ref, bbre_ref, bbim_ref):
    lre = lre_ref[...]
    lim = lim_ref[...]
    dt = jnp.exp(ldt_ref[...])
    mag = jnp.exp(lre * dt)
    a_re = mag * jnp.cos(lim * dt)
    a_im = mag * jnp.sin(lim * dt)
    den = lre * lre + lim * lim
    n_re = a_re - 1.0
    f_re = ((n_re * lre + a_im * lim) / den)[:, None, :]
    f_im = ((a_im * lre - n_re * lim) / den)[:, None, :]
    are_ref[...] = a_re
    aim_ref[...] = a_im
    bre = bre_ref[...]
    bim = bim_ref[...]
    bbre_ref[...] = f_re * bre - f_im * bim
    bbim_ref[...] = f_re * bim + f_im * bre


def _s5_prepare(lam_re, lam_im, log_dt, b_re, b_im, c_re, c_im, B):
    depth, _, G, P = lam_re.shape
    H = b_re.shape[-1]
    n = depth * 2 * G
    bT_re = jnp.swapaxes(b_re, -1, -2).reshape(n, H, P)
    bT_im = jnp.swapaxes(b_im, -1, -2).reshape(n, H, P)
    a_re, a_im, bb_re, bb_im = pl.pallas_call(
        _s5_prep_kernel,
        out_shape=[jax.ShapeDtypeStruct((n, P), F32), jax.ShapeDtypeStruct((n, P), F32),
                   jax.ShapeDtypeStruct((n, H, P), F32), jax.ShapeDtypeStruct((n, H, P), F32)],
        name="s5_discretize",
    )(lam_re.reshape(n, P), lam_im.reshape(n, P), log_dt.reshape(n, 1), bT_re, bT_im)
    gh = G // 2
    eye = jnp.eye(gh, dtype=F32)

    def bmat(bb):
        x = bb.reshape(depth, 2, 2, gh, H, P)
        return jnp.einsum("ldkghp,gj->ldkghjp", x, eye).reshape(depth, 2, 2, gh * H, gh * P)

    def cmat(c):
        x = c.astype(F32).reshape(depth, 2, 2, gh, H, P)
        return jnp.einsum("ldkghp,gj->ldkgpjh", x, eye).reshape(depth, 2, 2, gh * P, gh * H)

    bm = jnp.concatenate([bmat(bb_re), bmat(bb_im)], axis=-1).astype(BF16)
    cm = jnp.concatenate([cmat(c_re), -cmat(c_im)], axis=-2).astype(BF16)
    a_re = jnp.tile(a_re.reshape(depth, 2, 2, gh * P), (1, 1, B, 1))
    a_im = jnp.tile(a_im.reshape(depth, 2, 2, gh * P), (1, 1, B, 1))
    return bm, cm, a_re, a_im


def _s5_kernel(uf_ref, ub_ref, bm_ref, cm_ref, are_ref, aim_ref, yf_ref, yb_ref, bu_ref, h_ref, *, B):
    T, pitch, shift = S5_T, S5_PITCH, S5_SHIFT
    R = 2 * B
    wide = T + SUBLANES
    hw = bm_ref.shape[2]
    sw = are_ref.shape[2]
    nsl = sw // LANES
    u_refs = (uf_ref, ub_ref)
    y_refs = (yf_ref, yb_ref)

    @pl.when(pl.program_id(0) == 0)
    def _():
        h_ref[...] = jnp.zeros_like(h_ref)

    def window(b, k):
        r = 2 * b + k
        return (r * pitch - shift, wide) if k else (r * pitch, T)

    for d in range(2):
        for k in range(2):
            parts = []
            for b in range(B):
                ub = u_refs[d][b, :, k * hw:(k + 1) * hw]
                if k:
                    ub = pltpu.roll(jnp.concatenate([ub, jnp.zeros((SUBLANES, hw), F32)], axis=0), shift, axis=0)
                parts.append(ub)
            bu = _dot(jnp.concatenate(parts, axis=0).astype(BF16), bm_ref[d, k])
            rows = wide if k else T
            for b in range(B):
                base, _ = window(b, k)
                for sl in range(2 * nsl):
                    bu_ref[d, sl, base:base + rows, :] = bu[b * rows:(b + 1) * rows, sl * LANES:(sl + 1) * LANES]

    def step(d, t, h_re, h_im):
        idx = pl.ds(t, R, stride=pitch)
        b_re = jnp.concatenate([bu_ref[d, sl, idx, :] for sl in range(nsl)], axis=1)
        b_im = jnp.concatenate([bu_ref[d, nsl + sl, idx, :] for sl in range(nsl)], axis=1)
        a_re = are_ref[d]
        a_im = aim_ref[d]
        n_re = a_re * h_re - a_im * h_im + b_re
        n_im = a_re * h_im + a_im * h_re + b_im
        for sl in range(nsl):
            bu_ref[d, sl, idx, :] = n_re[:, sl * LANES:(sl + 1) * LANES]
            bu_ref[d, nsl + sl, idx, :] = n_im[:, sl * LANES:(sl + 1) * LANES]
        return n_re, n_im

    def body(j, carry):
        f_re, f_im, r_re, r_im = carry
        f_re, f_im = step(0, j, f_re, f_im)
        r_re, r_im = step(1, T - 1 - j, r_re, r_im)
        return f_re, f_im, r_re, r_im

    carry = lax.fori_loop(0, T, body, (h_ref[0, 0], h_ref[0, 1], h_ref[1, 0], h_ref[1, 1]), unroll=2)
    h_ref[0, 0] = carry[0]
    h_ref[0, 1] = carry[1]
    h_ref[1, 0] = carry[2]
    h_ref[1, 1] = carry[3]

    for d in range(2):
        for k in range(2):
            parts = []
            for b in range(B):
                base, rows = window(b, k)
                parts.append(jnp.concatenate([bu_ref[d, sl, base:base + rows, :] for sl in range(2 * nsl)], axis=1))
            y = _dot(jnp.concatenate(parts, axis=0).astype(BF16), cm_ref[d, k])
            for b in range(B):
                _, rows = window(b, k)
                yb = y[b * rows:(b + 1) * rows]
                if k:
                    yb = pltpu.roll(yb, rows - shift, axis=0)[:T]
                y_refs[d][b, :, k * hw:(k + 1) * hw] = yb


def _s5_scan(layer, su, bm, cm, a_re, a_im, dims):
    B, n_c, L = dims["B"], dims["n_c"], dims["L"]
    T = S5_T
    S = n_c + L
    ssm_w = su.shape[2]
    ncb, nlb = n_c // T, L // T
    R = 2 * B
    sw = a_re.shape[3]
    nslab = 2 * sw // LANES

    def bwd(s):
        return jnp.where(s < ncb, ncb - 1 - s, ncb + (nlb - 1 - (s - ncb)))

    blk = (B, T, ssm_w)
    kern = functools.partial(_s5_kernel, B=B)
    return pl.pallas_call(
        kern,
        grid=(S // T,),
        in_specs=[
            pl.BlockSpec(blk, lambda s: (0, s, 0)),
            pl.BlockSpec(blk, lambda s: (0, bwd(s), 0)),
            _resident((None,) + bm.shape[1:], lambda s: (layer, 0, 0, 0, 0)),
            _resident((None,) + cm.shape[1:], lambda s: (layer, 0, 0, 0, 0)),
            _resident((None,) + a_re.shape[1:], lambda s: (layer, 0, 0, 0)),
            _resident((None,) + a_im.shape[1:], lambda s: (layer, 0, 0, 0)),
        ],
        out_specs=[pl.BlockSpec(blk, lambda s: (0, s, 0)), pl.BlockSpec(blk, lambda s: (0, bwd(s), 0))],
        out_shape=[jax.ShapeDtypeStruct(su.shape, F32), jax.ShapeDtypeStruct(su.shape, F32)],
        scratch_shapes=[
            pltpu.VMEM((2, nslab, R * S5_PITCH, LANES), F32),
            pltpu.VMEM((2, 2, R, sw), F32),
        ],
        compiler_params=_cparams(("arbitrary",)),
        name="s5_scan",
    )(su, su, bm, cm, a_re, a_im)


def _mix_out_kernel(x_ref, oa_ref, yf_ref, yb_ref, su_ref, gm_ref, d_ref, wg_ref, bg_ref, lg_ref, lb_ref,
                    ws_ref, bs_ref, wo_ref, gate_ref, gf_ref, shf_ref, scf_ref, o_ref, h_ref):
    y = d_ref[...] * su_ref[...] + yf_ref[...] + yb_ref[...]
    g1 = _gelu(y)
    o_ssm = g1 * _sigmoid(_dot(g1.astype(BF16), wg_ref[...]) + bg_ref[...])
    gw = gm_ref.shape[1] // 2
    u = _gelu(gm_ref[:, :gw])
    v = _gelu(gm_ref[:, gw:])
    mu = jnp.mean(v, axis=-1, keepdims=True)
    var = jnp.mean(jnp.square(v - mu), axis=-1, keepdims=True)
    v = ((v - mu) * lax.rsqrt(var + NORM_EPS) * lg_ref[...] + lb_ref[...]).astype(BF16)
    tm = u.shape[0]
    bs = bs_ref[...]
    rows = []
    for c in range(tm // GMLP_CHUNK):
        cols = []
        for g in range(gw // GMLP_GROUP_W):
            vb = v[c * GMLP_CHUNK:(c + 1) * GMLP_CHUNK, g * GMLP_GROUP_W:(g + 1) * GMLP_GROUP_W]
            cols.append(_dot(ws_ref[g], vb) + bs[:, g:g + 1])
        rows.append(jnp.concatenate(cols, axis=1))
    o_gmlp = u * jnp.concatenate(rows, axis=0)
    mix = jnp.concatenate([oa_ref[...], o_ssm.astype(BF16), o_gmlp.astype(BF16)], axis=1)
    x1 = x_ref[...] + gate_ref[...] * _dot(mix, wo_ref[...])
    o_ref[...] = x1
    h_ref[...] = _norm_mod(x1, gf_ref[...], shf_ref[...], scf_ref[...]).astype(BF16)


def _mix_out(layer, xall, o_attn, y_f, y_b, su, gm, p, g_ffn, mod4, dims, skip_ctx):
    B, n_c, L, ntok = dims["B"], dims["n_c"], dims["L"], dims["ntok"]
    d = xall.shape[1]
    tm = n_c
    lt = L // tm
    attn_w, ssm_w, gm_w = o_attn.shape[1], su.shape[2], gm.shape[1]
    ngrp = p["w_s"].shape[1]
    off = B if skip_ctx else 0
    nrow = ntok - off * tm
    assert o_attn.shape[0] == nrow

    def row(i):
        return jnp.where(i + off < B, B, (i + off - B) // lt)

    def seq(i):
        i = i + off
        return (jnp.where(i < B, i, (i - B) // lt), jnp.where(i < B, 0, 1 + (i - B) % lt), 0)

    vec = lambda w: pl.BlockSpec((None, 1, w), lambda i: (layer, 0, 0))
    modspec = lambda k: pl.BlockSpec((None, None, 1, d), lambda i: (layer, row(i), 0, k))
    return pl.pallas_call(
        _mix_out_kernel,
        grid=(nrow // tm,),
        in_specs=[
            pl.BlockSpec((tm, d), lambda i: (i + off, 0)),
            pl.BlockSpec((tm, attn_w), lambda i: (i, 0)),
            pl.BlockSpec((None, tm, ssm_w), seq),
            pl.BlockSpec((None, tm, ssm_w), seq),
            pl.BlockSpec((None, tm, ssm_w), seq),
            pl.BlockSpec((tm, gm_w), lambda i: (i + off, 0)),
            vec(ssm_w),
            _resident((None, ssm_w, ssm_w), lambda i: (layer, 0, 0)),
            vec(ssm_w), vec(gm_w // 2), vec(gm_w // 2),
            _resident((None, ngrp, GMLP_CHUNK, GMLP_CHUNK), lambda i: (layer, 0, 0, 0)),
            pl.BlockSpec((None, GMLP_CHUNK, ngrp), lambda i: (layer, 0, 0)),
            _resident((None, d, d), lambda i: (layer, 0, 0)),
            modspec(2), vec(d), modspec(3), modspec(4),
        ],
        out_specs=[pl.BlockSpec((tm, d), lambda i: (i, 0)), pl.BlockSpec((tm, d), lambda i: (i, 0))],
        out_shape=[jax.ShapeDtypeStruct((nrow, d), F32), jax.ShapeDtypeStruct((nrow, d), BF16)],
        compiler_params=_cparams(("arbitrary",)),
        name="mix_out",
    )(xall, o_attn, y_f, y_b, su, gm, p["ssm_d"], p["w_glu"], p["b_glu"], p["ln_g"], p["ln_b"],
      p["w_s"], p["b_s_t"], p["w_out"], mod4, g_ffn, mod4, mod4)


def _ffn_kernel(hm_ref, hp_ref, hn_ref, x_ref, gate_ref, gfin_ref, wg_ref, wv_ref, cw_ref, cb_ref, wd_ref,
                o_ref, h_ref, mp_ref, mn_ref, *, n_ctx_tiles, n_c, L, final):
    i = pl.program_id(0)
    j = pl.program_id(1)
    tm = x_ref.shape[0]
    halo = hp_ref.shape[0]

    @pl.when(j == 0)
    def _():
        h_ref[0:halo, :] = hp_ref[...]
        h_ref[halo:halo + tm, :] = hm_ref[...]
        h_ref[halo + tm:, :] = hn_ref[...]
        is_ctx = i < n_ctx_tiles
        seq_len = jnp.where(is_ctx, n_c, L)
        start = lax.rem(jnp.where(is_ctx, i * tm, (i - n_ctx_tiles) * tm), seq_len)
        pos = start + lax.broadcasted_iota(jnp.int32, (tm, LANES), 0)
        first = pos == 0
        last = pos == seq_len - 1
        for k in range(1, tm // min(n_c, L) + 1):
            first = jnp.logical_or(first, pos == k * seq_len)
            last = jnp.logical_or(last, pos == (k + 1) * seq_len - 1)
        mp_ref[...] = jnp.where(first, 0.0, 1.0)
        mn_ref[...] = jnp.where(last, 0.0, 1.0)
        o_ref[...] = jnp.zeros_like(o_ref)

    tf = wg_ref.shape[1]
    rep = tf // LANES
    ge = _dot(h_ref[...], wg_ref[...])
    val = _dot(h_ref[halo:halo + tm, :], wv_ref[...])
    ext = tm + 2 * halo
    g_prev = pltpu.roll(ge, 1, axis=0)[halo:halo + tm]
    g_next = pltpu.roll(ge, ext - 1, axis=0)[halo:halo + tm]
    mp = jnp.concatenate([mp_ref[...]] * rep, axis=1)
    mn = jnp.concatenate([mn_ref[...]] * rep, axis=1)
    cw = cw_ref[...]
    gc = cw[0:1] * (g_prev * mp) + cw[1:2] * ge[halo:halo + tm] + cw[2:3] * (g_next * mn) + cb_ref[...]
    act = (gc * _sigmoid(gc) * val).astype(BF16)
    o_ref[...] += _dot(act, wd_ref[...])

    @pl.when(j == pl.num_programs(1) - 1)
    def _():
        xn = x_ref[...] + gate_ref[...] * o_ref[...]
        if final:
            xn = xn * lax.rsqrt(jnp.mean(xn * xn, axis=-1, keepdims=True) + NORM_EPS) * gfin_ref[...]
        o_ref[...] = xn


def _conv_ffn(layer, x1, h2, mod4, w_up, conv_w, conv_b, w_down, g_final, dims, skip_ctx, final):
    B, n_c, L = dims["B"], dims["n_c"], dims["L"]
    nrow, d = x1.shape
    d_ff = w_down.shape[1]
    tm, tf = dims["ffn_tm"], FFN_TF
    halo = BF16_ROWS
    nf = d_ff // tf
    nct = 0 if skip_ctx else B * n_c // tm
    lt = L // tm
    nhalo = nrow // halo

    def row(i):
        return jnp.where(i < nct, B, (i - nct) // lt)

    modspec = lambda k: pl.BlockSpec((None, None, 1, d), lambda i, j: (layer, row(i), 0, k))
    kern = functools.partial(_ffn_kernel, n_ctx_tiles=nct, n_c=n_c, L=L, final=final)
    return pl.pallas_call(
        kern,
        grid=(nrow // tm, nf),
        in_specs=[
            pl.BlockSpec((tm, d), lambda i, j: (i, 0)),
            pl.BlockSpec((halo, d), lambda i, j: (jnp.maximum(i * (tm // halo) - 1, 0), 0)),
            pl.BlockSpec((halo, d), lambda i, j: (jnp.minimum((i + 1) * (tm // halo), nhalo - 1), 0)),
            pl.BlockSpec((tm, d), lambda i, j: (i, 0)),
            modspec(5),
            pl.BlockSpec((1, d), lambda i, j: (0, 0)),
            pl.BlockSpec((None, d, tf), lambda i, j: (layer, 0, j)),
            pl.BlockSpec((None, d, tf), lambda i, j: (layer, 0, nf + j)),
            pl.BlockSpec((None, 3, tf), lambda i, j: (layer, 0, j)),
            pl.BlockSpec((None, 1, tf), lambda i, j: (layer, 0, j)),
            pl.BlockSpec((None, tf, d), lambda i, j: (layer, j, 0)),
        ],
        out_specs=pl.BlockSpec((tm, d), lambda i, j: (i, 0)),
        out_shape=jax.ShapeDtypeStruct(x1.shape, F32),
        scratch_shapes=[
            pltpu.VMEM((tm + 2 * halo, d), BF16),
            pltpu.VMEM((tm, LANES), F32),
            pltpu.VMEM((tm, LANES), F32),
        ],
        compiler_params=_cparams(("arbitrary", "arbitrary")),
        name="conv_ffn",
    )(h2, h2, h2, x1, mod4, g_final, w_up, w_up, conv_w, conv_b, w_down)


def _rope_tables(n_c, L):
    n_freq = HEAD_DIM // 4
    t = np.arange(L)
    inv_freq = jnp.asarray(ROPE_BASE, F32) ** (-jnp.arange(n_freq, dtype=F32) / n_freq)
    pos = jnp.asarray(np.stack([t // GRID_W, t % GRID_W], axis=-1), F32)
    ang = pos[:, :, None] * inv_freq
    cos, sin = jnp.cos(ang), jnp.sin(ang)
    c64 = jnp.concatenate([cos, cos], axis=-1).reshape(L, HEAD_DIM)
    s64 = jnp.concatenate([-sin, sin], axis=-1).reshape(L, HEAD_DIM)
    reps = LANES // HEAD_DIM
    c = jnp.concatenate([jnp.ones((n_c, LANES), F32), jnp.tile(c64, (1, reps))], axis=0)
    s = jnp.concatenate([jnp.zeros((n_c, LANES), F32), jnp.tile(s64, (1, reps))], axis=0)
    return c, s


def kernel(x, c, ctx, c_ctx, w_ada, b_ada, g_mix, g_ffn, w_in, w_out, attn_sink, ssm_lambda_re, ssm_lambda_im,
           ssm_log_dt, ssm_b_re, ssm_b_im, ssm_c_re, ssm_c_im, ssm_d, ssm_w_glu, ssm_b_glu, gmlp_ln_g, gmlp_ln_b,
           gmlp_w_s, gmlp_b_s, ffn_w_up, ffn_conv_w, ffn_conv_b, ffn_w_down, g_final):
    B, L, d = x.shape
    n_c = ctx.shape[1]
    depth = w_in.shape[0]
    ssm_w = ssm_d.shape[1]
    gmlp_w = gmlp_ln_g.shape[1]
    attn_w = w_in.shape[2] - ssm_w - 2 * gmlp_w
    attn_w = attn_w * GQA_RATIO // (GQA_RATIO + 2)
    kv_w = attn_w // GQA_RATIO
    ntok = B * (n_c + L)
    ffn_tm = min(FFN_TM, B * n_c)
    assert n_c % ATTN_BLOCK == 0 and L % n_c == 0 and L % ffn_tm == 0 and (B * n_c) % ffn_tm == 0
    assert n_c % S5_T == 0 and L % S5_T == 0 and B + 1 <= SUBLANES
    dims = dict(B=B, n_c=n_c, L=L, ntok=ntok, attn_w=attn_w, kv_w=kv_w, ssm_w=ssm_w, ffn_tm=ffn_tm)

    cpad = jnp.zeros((SUBLANES, d), F32).at[:B].set(c).at[B].set(c_ctx)
    mod = _modulation(cpad, w_ada, b_ada)
    mod4 = mod.reshape(depth, SUBLANES, 1, N_MOD * d)

    bm, cm, a_re, a_im = _s5_prepare(ssm_lambda_re, ssm_lambda_im, ssm_log_dt, ssm_b_re, ssm_b_im,
                                     ssm_c_re, ssm_c_im, B)
    cos_t, sin_t = _rope_tables(n_c, L)
    bias = _attn_bias(n_c)

    w_in_b = w_in.astype(BF16)
    w_up_b = ffn_w_up.astype(BF16)
    w_down_b = ffn_w_down.astype(BF16)
    mixp = dict(
        ssm_d=ssm_d.reshape(depth, 1, ssm_w), w_glu=ssm_w_glu.astype(BF16), b_glu=ssm_b_glu.reshape(depth, 1, ssm_w),
        ln_g=gmlp_ln_g.reshape(depth, 1, gmlp_w), ln_b=gmlp_ln_b.reshape(depth, 1, gmlp_w),
        w_s=gmlp_w_s.astype(BF16), b_s_t=jnp.swapaxes(gmlp_b_s, 1, 2), w_out=w_out.astype(BF16))
    g_mix3 = g_mix.reshape(depth, 1, d)
    g_ffn3 = g_ffn.reshape(depth, 1, d)
    conv_b3 = ffn_conv_b.reshape(depth, 1, -1)

    xall = jnp.concatenate([ctx.reshape(B * n_c, d), x.reshape(B * L, d)], axis=0)
    g_fin = g_final.reshape(1, d)
    for layer in range(depth):
        last = layer == depth - 1
        q, kx, vx, su, gm = _in_proj(layer, xall, g_mix3, mod4, w_in_b, cos_t, sin_t, dims)
        o_attn = _attention(layer, q, kx, vx, attn_sink, bias, dims, skip_ctx=last)
        y_f, y_b = _s5_scan(layer, su, bm, cm, a_re, a_im, dims)
        x1, h2 = _mix_out(layer, xall, o_attn, y_f, y_b, su, gm, mixp, g_ffn3, mod4, dims, skip_ctx=last)
        xall = _conv_ffn(layer, x1, h2, mod4, w_up_b, ffn_conv_w, conv_b3, w_down_b, g_fin, dims,
                         skip_ctx=last, final=last)
    return xall.reshape(B, L, d)
```

```python
import functools
import math

import numpy as np
import jax
import jax.numpy as jnp
from jax import lax
from jax.experimental import pallas as pl
from jax.experimental.pallas import tpu as pltpu

F32 = jnp.float32
BF16 = jnp.bfloat16

HEAD_DIM = 64
GQA_RATIO = 8
GRID_W = 64
ROPE_BASE = 10000.0
ATTN_BLOCK = 128
SSM_GROUP = 16
SSM_STATE = 64
GMLP_CHUNK = 128
GMLP_GROUP_W = 128
N_MOD = 6
NORM_EPS = 1e-6
MASK_NEG = -1e30
LOG2E = 1.4426950408889634

LANES = 128
SUBLANES = 8
BF16_ROWS = 16

MOD_TN = 1024
FFN_TM = 512
FFN_TF = 512
ATTN_STACK = 4
S5_T = 128
S5_PITCH = S5_T + 4
S5_SHIFT = 4
VMEM_LIMIT = 56 * 1024 * 1024


def _cparams(sem):
    return pltpu.CompilerParams(dimension_semantics=sem, vmem_limit_bytes=VMEM_LIMIT)


def _sigmoid(x):
    return 1.0 / (1.0 + jnp.exp(-x))


def _gelu(x):
    return 0.5 * x * (1.0 + jnp.tanh(0.7978845608028654 * (x + 0.044715 * (x * x * x))))


def _norm_mod(x, g, shift, scale):
    y = x * lax.rsqrt(jnp.mean(x * x, axis=-1, keepdims=True) + NORM_EPS)
    return y * (g * (1.0 + scale)) + shift


def _dot(a, b):
    return jnp.dot(a, b, preferred_element_type=F32)


def _resident(block_shape, index_map):
    return pl.BlockSpec(block_shape, index_map, pipeline_mode=pl.Buffered(1))


def _mod_kernel(c_ref, w_ref, b_ref, o_ref):
    c = c_ref[...]
    act = (c * _sigmoid(c)).astype(BF16)
    o_ref[...] = _dot(act, w_ref[...].astype(BF16)) + b_ref[...]


def _modulation(cpad, w_ada, b_ada):
    depth, d, n = w_ada.shape
    rows = cpad.shape[0]
    return pl.pallas_call(
        _mod_kernel,
        grid=(depth, n // MOD_TN),
        in_specs=[
            pl.BlockSpec((rows, d), lambda l, j: (0, 0)),
            pl.BlockSpec((None, d, MOD_TN), lambda l, j: (l, 0, j)),
            pl.BlockSpec((None, 1, MOD_TN), lambda l, j: (l, 0, j)),
        ],
        out_specs=pl.BlockSpec((None, rows, MOD_TN), lambda l, j: (l, 0, j)),
        out_shape=jax.ShapeDtypeStruct((depth, rows, n), F32),
        compiler_params=_cparams(("arbitrary", "arbitrary")),
        name="adaln_mod",
    )(cpad, w_ada, b_ada.reshape(depth, 1, n))


def _rope(x, c, s, lane):
    first = (lane % 32) < 16
    partner = jnp.where(first, pltpu.roll(x, LANES - 16, axis=1), pltpu.roll(x, 16, axis=1))
    return x * c + partner * s


def _x_specs(xs, tm, n_ctx_tiles, off=0):
    d = xs[0].shape[1]
    if len(xs) == 1:
        return [pl.BlockSpec((tm, d), lambda i: (i + off, 0))]
    return [pl.BlockSpec((tm, d), lambda i: (jnp.minimum(i + off, n_ctx_tiles - 1), 0)),
            pl.BlockSpec((tm, d), lambda i: (jnp.maximum(i + off - n_ctx_tiles, 0), 0))]


def _read_x(x_refs, n_ctx_tiles, off=0):
    if len(x_refs) == 1:
        return x_refs[0][...]
    return jnp.where(pl.program_id(0) + off < n_ctx_tiles, x_refs[0][...], x_refs[1][...])


def _in_proj_kernel(*refs, attn_w, kv_w, ssm_w, n_x, n_ctx_tiles):
    x = _read_x(refs[:n_x], n_ctx_tiles)
    g_ref, sh_ref, sc_ref, w_ref, cos_ref, sin_ref, q_ref, kx_ref, vx_ref, su_ref, gm_ref = refs[n_x:]
    h = _norm_mod(x, g_ref[...], sh_ref[...], sc_ref[...]).astype(BF16)
    z = _dot(h, w_ref[...])
    tm = z.shape[0]
    c = cos_ref[...]
    s = sin_ref[...]
    lane = lax.broadcasted_iota(jnp.int32, (tm, LANES), 1)
    qscale = HEAD_DIM ** -0.5 * LOG2E
    for j in range(attn_w // LANES):
        zq = z[:, j * LANES:(j + 1) * LANES]
        q_ref[:, j * LANES:(j + 1) * LANES] = (_rope(zq, c, s, lane) * qscale).astype(BF16)
    low = lane < HEAD_DIM
    off_k, off_v = attn_w, attn_w + kv_w
    for src, dst, rot in ((z[:, off_k:off_k + kv_w], kx_ref, True), (z[:, off_v:off_v + kv_w], vx_ref, False)):
        t = _rope(src, c, s, lane) if rot else src
        h0 = jnp.where(low, t, 0.0)
        h1 = jnp.where(low, 0.0, t)
        parts = (h0, pltpu.roll(h0, HEAD_DIM, axis=1), pltpu.roll(h1, HEAD_DIM, axis=1), h1)
        for p, val in enumerate(parts):
            dst[:, p * LANES:(p + 1) * LANES] = val.astype(BF16)
    off_s = attn_w + 2 * kv_w
    su_ref[...] = z[:, off_s:off_s + ssm_w]
    gm_ref[...] = z[:, off_s + ssm_w:]


def _in_proj(layer, xs, g_mix, mod4, w_in, cos_t, sin_t, dims):
    B, n_c, L, ntok = dims["B"], dims["n_c"], dims["L"], dims["ntok"]
    d = xs[0].shape[1]
    n_in = w_in.shape[2]
    attn_w, kv_w, ssm_w = dims["attn_w"], dims["kv_w"], dims["ssm_w"]
    gm_w = n_in - attn_w - 2 * kv_w - ssm_w
    tm = n_c
    lt = L // tm
    S = n_c + L

    def row(i):
        return jnp.where(i < B, B, (i - B) // lt)

    def tbl(i):
        return jnp.where(i < B, 0, 1 + (i - B) % lt)

    def seq(i):
        return (jnp.where(i < B, i, (i - B) // lt), jnp.where(i < B, 0, 1 + (i - B) % lt), 0)

    kern = functools.partial(_in_proj_kernel, attn_w=attn_w, kv_w=kv_w, ssm_w=ssm_w, n_x=len(xs), n_ctx_tiles=B)
    return pl.pallas_call(
        kern,
        grid=(ntok // tm,),
        in_specs=_x_specs(xs, tm, B) + [
            pl.BlockSpec((None, 1, d), lambda i: (layer, 0, 0)),
            pl.BlockSpec((None, None, 1, d), lambda i: (layer, row(i), 0, 0)),
            pl.BlockSpec((None, None, 1, d), lambda i: (layer, row(i), 0, 1)),
            _resident((None, d, n_in), lambda i: (layer, 0, 0)),
            pl.BlockSpec((tm, LANES), lambda i: (tbl(i), 0)),
            pl.BlockSpec((tm, LANES), lambda i: (tbl(i), 0)),
        ],
        out_specs=[
            pl.BlockSpec((tm, attn_w), lambda i: (i, 0)),
            pl.BlockSpec((tm, 4 * LANES), lambda i: (i, 0)),
            pl.BlockSpec((tm, 4 * LANES), lambda i: (i, 0)),
            pl.BlockSpec((None, tm, ssm_w), seq),
            pl.BlockSpec((tm, gm_w), lambda i: (i, 0)),
        ],
        out_shape=[
            jax.ShapeDtypeStruct((ntok, attn_w), BF16),
            jax.ShapeDtypeStruct((ntok, 4 * LANES), BF16),
            jax.ShapeDtypeStruct((ntok, 4 * LANES), BF16),
            jax.ShapeDtypeStruct((B, S, ssm_w), F32),
            jax.ShapeDtypeStruct((ntok, gm_w), F32),
        ],
        compiler_params=_cparams(("arbitrary",)),
        name="in_proj",
    )(*xs, g_mix, mod4, mod4, w_in, cos_t, sin_t)


def _attn_kernel(sink_ref, q_ref, kp_ref, ko_ref, kn_ref, kc_ref, vp_ref, vo_ref, vn_ref, vc_ref,
                 bias_ref, o_ref, s_ref, p_ref, *, layer, n_kv):
    blk = q_ref.shape[0]
    per_kv = GQA_RATIO * HEAD_DIM // LANES
    bias = bias_ref[...]
    nkeys = bias.shape[1]
    low = lax.broadcasted_iota(jnp.int32, (blk, LANES), 1) < HEAD_DIM
    krefs = (kp_ref, ko_ref, kn_ref, kc_ref)
    vrefs = (vp_ref, vo_ref, vn_ref, vc_ref)
    for g in range(n_kv):
        def keys(refs, half):
            lo = (2 * g + half) * LANES
            return [r[:, lo:lo + LANES] for r in refs]
        kb = jnp.concatenate(keys(krefs, 0) + keys(krefs, 1), axis=0)
        vb = jnp.concatenate(keys(vrefs, 0) + keys(vrefs, 1), axis=0)
        for c0 in range(0, per_kv, ATTN_STACK):
            ch = g * (per_kv // ATTN_STACK) + c0 // ATTN_STACK
            cols = [(g * per_kv + c0 + c) * LANES for c in range(ATTN_STACK)]
            qs = jnp.concatenate([q_ref[:, col:col + LANES] for col in cols], axis=0)
            s_ref[ch] = lax.dot_general(qs, kb, (((1,), (1,)), ((), ())), preferred_element_type=F32)
            scales = []
            for c in range(ATTN_STACK):
                rows = slice(c * blk, (c + 1) * blk)
                invs = []
                for p in range(2):
                    sink = sink_ref[layer, g * GQA_RATIO + 2 * (c0 + c) + p] * LOG2E
                    slabs = [slice(p * nkeys + k * LANES, p * nkeys + (k + 1) * LANES) for k in range(nkeys // LANES)]
                    mx = None
                    for k, sl in enumerate(slabs):
                        sk = s_ref[ch, rows, sl] + bias[:, k * LANES:(k + 1) * LANES]
                        mx = sk if mx is None else jnp.maximum(mx, sk)
                    m = jnp.maximum(jnp.max(mx, axis=1, keepdims=True), sink)
                    acc = None
                    for k, sl in enumerate(slabs):
                        e = jnp.exp2(s_ref[ch, rows, sl] + bias[:, k * LANES:(k + 1) * LANES] - m)
                        p_ref[ch, rows, sl] = e.astype(BF16)
                        acc = e if acc is None else acc + e
                    den = jnp.sum(acc, axis=1, keepdims=True) + jnp.exp2(sink - m)
                    invs.append(1.0 / den)
                scales.append(jnp.where(low, invs[0], invs[1]))
            o = _dot(p_ref[ch], vb)
            for c, col in enumerate(cols):
                o_ref[:, col:col + LANES] = (o[c * blk:(c + 1) * blk] * scales[c]).astype(o_ref.dtype)


def _attn_bias(n_c):
    blk = ATTN_BLOCK
    r = np.arange(blk)[:, None]
    j = np.arange(blk)[None, :]
    zero = np.zeros((blk, blk), np.float32)
    neg = np.full((blk, blk), MASK_NEG, np.float32)
    prev = np.where(j >= r, 0.0, MASK_NEG).astype(np.float32)
    nxt = np.where(j <= r, 0.0, MASK_NEG).astype(np.float32)
    ctx = np.zeros((blk, n_c), np.float32)
    variants = [
        np.concatenate([neg, zero, nxt, ctx], 1),
        np.concatenate([prev, zero, nxt, ctx], 1),
        np.concatenate([prev, zero, neg, ctx], 1),
        np.concatenate([neg, neg, neg, ctx], 1),
    ]
    return jnp.asarray(np.stack(variants))


def _attention(layer, q, kx, vx, sink, bias, dims, skip_ctx):
    B, n_c, L, ntok = dims["B"], dims["n_c"], dims["L"], dims["ntok"]
    blk = ATTN_BLOCK
    attn_w = q.shape[1]
    n_kv = dims["kv_w"] // HEAD_DIM
    nblk = ntok // blk
    nctb = B * n_c // blk
    cpb = n_c // blk
    nb = L // blk
    assert nb >= 2
    off = nctb if skip_ctx else 0

    def bat(t):
        return jnp.where(t < nctb, t // cpb, (t - nctb) // nb)

    def variant(t):
        n = (t - nctb) % nb
        return jnp.where(t < nctb, 3, jnp.where(n == 0, 0, jnp.where(n == nb - 1, 2, 1)))

    kvw = kx.shape[1]
    own = pl.BlockSpec((blk, kvw), lambda t: (t + off, 0))
    prev = pl.BlockSpec((blk, kvw), lambda t: (jnp.maximum(t + off - 1, 0), 0))
    nxt = pl.BlockSpec((blk, kvw), lambda t: (jnp.minimum(t + off + 1, nblk - 1), 0))
    ctx = pl.BlockSpec((n_c, kvw), lambda t: (bat(t + off), 0))
    kern = functools.partial(_attn_kernel, layer=layer, n_kv=n_kv)
    return pl.pallas_call(
        kern,
        grid=(nblk - off,),
        in_specs=[
            pl.BlockSpec(memory_space=pltpu.SMEM),
            pl.BlockSpec((blk, attn_w), lambda t: (t + off, 0)),
            prev, own, nxt, ctx, prev, own, nxt, ctx,
            pl.BlockSpec((None,) + bias.shape[1:], lambda t: (variant(t + off), 0, 0)),
        ],
        out_specs=pl.BlockSpec((blk, attn_w), lambda t: (t, 0)),
        out_shape=jax.ShapeDtypeStruct(((nblk - off) * blk, attn_w), BF16),
        scratch_shapes=[
            pltpu.VMEM((attn_w // LANES // ATTN_STACK, ATTN_STACK * blk, 2 * bias.shape[2]), F32),
            pltpu.VMEM((attn_w // LANES // ATTN_STACK, ATTN_STACK * blk, 2 * bias.shape[2]), BF16),
        ],
        compiler_params=_cparams(("arbitrary",)),
        name="window_attn",
    )(sink, q, kx, kx, kx, kx, vx, vx, vx, vx, bias)


def _s5_prep_kernel(lre_ref, lim_ref, ldt_ref, bre_ref, bim_ref, are_ref, aim_ref, bbre_ref, bbim_ref):
    lre = lre_ref[...]
    lim = lim_ref[...]
    dt = jnp.exp(ldt_ref[...])
    mag = jnp.exp(lre * dt)
    a_re = mag * jnp.cos(lim * dt)
    a_im = mag * jnp.sin(lim * dt)
    den = lre * lre + lim * lim
    n_re = a_re - 1.0
    f_re = ((n_re * lre + a_im * lim) / den)[:, None, :]
    f_im = ((a_im * lre - n_re * lim) / den)[:, None, :]
    are_ref[...] = a_re
    aim_ref[...] = a_im
    bre = bre_ref[...]
    bim = bim_ref[...]
    bbre_ref[...] = f_re * bre - f_im * bim
    bbim_ref[...] = f_re * bim + f_im * bre


def _s5_prepare(lam_re, lam_im, log_dt, b_re, b_im, c_re, c_im, B):
    depth, _, G, P = lam_re.shape
    H = b_re.shape[-1]
    n = depth * 2 * G
    bT_re = jnp.swapaxes(b_re, -1, -2).reshape(n, H, P)
    bT_im = jnp.swapaxes(b_im, -1, -2).reshape(n, H, P)
    a_re, a_im, bb_re, bb_im = pl.pallas_call(
        _s5_prep_kernel,
        out_shape=[jax.ShapeDtypeStruct((n, P), F32), jax.ShapeDtypeStruct((n, P), F32),
                   jax.ShapeDtypeStruct((n, H, P), F32), jax.ShapeDtypeStruct((n, H, P), F32)],
        name="s5_discretize",
    )(lam_re.reshape(n, P), lam_im.reshape(n, P), log_dt.reshape(n, 1), bT_re, bT_im)
    gh = G // 2
    eye = jnp.eye(gh, dtype=BF16)
    bb = jnp.stack([bb_re, bb_im], axis=2).astype(BF16).reshape(depth, 2, 2, gh, H, 2, P)
    bm = jnp.einsum("ldkghcp,gj->ldkghcjp", bb, eye).reshape(depth, 2, 2, gh * H, 2 * gh * P)
    cc = jnp.stack([c_re, -c_im], axis=4).astype(BF16).reshape(depth, 2, 2, gh, H, 2, P)
    cm = jnp.einsum("ldkghcp,gj->ldkcgpjh", cc, eye).reshape(depth, 2, 2, 2 * gh * P, gh * H)
    a_re = jnp.tile(a_re.reshape(depth, 2, 2, gh * P), (1, 1, B, 1))
    a_im = jnp.tile(a_im.reshape(depth, 2, 2, gh * P), (1, 1, B, 1))
    return bm, cm, a_re, a_im


def _s5_kernel(uf_ref, ub_ref, bm_ref, cm_ref, are_ref, aim_ref, yf_ref, yb_ref, bu_ref, h_ref, *, B):
    T, pitch, shift = S5_T, S5_PITCH, S5_SHIFT
    R = 2 * B
    wide = T + SUBLANES
    hw = bm_ref.shape[2]
    sw = are_ref.shape[2]
    nsl = sw // LANES
    u_refs = (uf_ref, ub_ref)
    y_refs = (yf_ref, yb_ref)

    @pl.when(pl.program_id(0) == 0)
    def _():
        h_ref[...] = jnp.zeros_like(h_ref)

    def window(b, k):
        r = 2 * b + k
        return (r * pitch - shift, wide) if k else (r * pitch, T)

    for d in range(2):
        for k in range(2):
            parts = []
            for b in range(B):
                ub = u_refs[d][b, :, k * hw:(k + 1) * hw]
                if k:
                    ub = pltpu.roll(jnp.concatenate([ub, jnp.zeros((SUBLANES, hw), F32)], axis=0), shift, axis=0)
                parts.append(ub)
            bu = _dot(jnp.concatenate(parts, axis=0).astype(BF16), bm_ref[d, k])
            rows = wide if k else T
            for b in range(B):
                base, _ = window(b, k)
                for sl in range(2 * nsl):
                    bu_ref[d, sl, base:base + rows, :] = bu[b * rows:(b + 1) * rows, sl * LANES:(sl + 1) * LANES]

    def step(d, t, h_re, h_im):
        idx = pl.ds(t, R, stride=pitch)
        b_re = jnp.concatenate([bu_ref[d, sl, idx, :] for sl in range(nsl)], axis=1)
        b_im = jnp.concatenate([bu_ref[d, nsl + sl, idx, :] for sl in range(nsl)], axis=1)
        a_re = are_ref[d]
        a_im = aim_ref[d]
        n_re = a_re * h_re - a_im * h_im + b_re
        n_im = a_re * h_im + a_im * h_re + b_im
        for sl in range(nsl):
            bu_ref[d, sl, idx, :] = n_re[:, sl * LANES:(sl + 1) * LANES]
            bu_ref[d, nsl + sl, idx, :] = n_im[:, sl * LANES:(sl + 1) * LANES]
        return n_re, n_im

    def body(j, carry):
        f_re, f_im, r_re, r_im = carry
        f_re, f_im = step(0, j, f_re, f_im)
        r_re, r_im = step(1, T - 1 - j, r_re, r_im)
        return f_re, f_im, r_re, r_im

    carry = lax.fori_loop(0, T, body, (h_ref[0, 0], h_ref[0, 1], h_ref[1, 0], h_ref[1, 1]), unroll=2)
    h_ref[0, 0] = carry[0]
    h_ref[0, 1] = carry[1]
    h_ref[1, 0] = carry[2]
    h_ref[1, 1] = carry[3]

    for d in range(2):
        for k in range(2):
            parts = []
            for b in range(B):
                base, rows = window(b, k)
                parts.append(jnp.concatenate([bu_ref[d, sl, base:base + rows, :] for sl in range(2 * nsl)], axis=1))
            y = _dot(jnp.concatenate(parts, axis=0).astype(BF16), cm_ref[d, k])
            for b in range(B):
                _, rows = window(b, k)
                yb = y[b * rows:(b + 1) * rows]
                if k:
                    yb = pltpu.roll(yb, rows - shift, axis=0)[:T]
                y_refs[d][b, :, k * hw:(k + 1) * hw] = yb


def _s5_scan(layer, su, bm, cm, a_re, a_im, dims):
    B, n_c, L = dims["B"], dims["n_c"], dims["L"]
    T = S5_T
    S = n_c + L
    ssm_w = su.shape[2]
    ncb, nlb = n_c // T, L // T
    R = 2 * B
    sw = a_re.shape[3]
    nslab = 2 * sw // LANES

    def bwd(s):
        return jnp.where(s < ncb, ncb - 1 - s, ncb + (nlb - 1 - (s - ncb)))

    blk = (B, T, ssm_w)
    kern = functools.partial(_s5_kernel, B=B)
    return pl.pallas_call(
        kern,
        grid=(S // T,),
        in_specs=[
            pl.BlockSpec(blk, lambda s: (0, s, 0)),
            pl.BlockSpec(blk, lambda s: (0, bwd(s), 0)),
            _resident((None,) + bm.shape[1:], lambda s: (layer, 0, 0, 0, 0)),
            _resident((None,) + cm.shape[1:], lambda s: (layer, 0, 0, 0, 0)),
            _resident((None,) + a_re.shape[1:], lambda s: (layer, 0, 0, 0)),
            _resident((None,) + a_im.shape[1:], lambda s: (layer, 0, 0, 0)),
        ],
        out_specs=[pl.BlockSpec(blk, lambda s: (0, s, 0)), pl.BlockSpec(blk, lambda s: (0, bwd(s), 0))],
        out_shape=[jax.ShapeDtypeStruct(su.shape, F32), jax.ShapeDtypeStruct(su.shape, F32)],
        scratch_shapes=[
            pltpu.VMEM((2, nslab, R * S5_PITCH, LANES), F32),
            pltpu.VMEM((2, 2, R, sw), F32),
        ],
        compiler_params=_cparams(("arbitrary",)),
        name="s5_scan",
    )(su, su, bm, cm, a_re, a_im)


def _mix_out_kernel(*refs, n_x, n_ctx_tiles, off):
    x_refs = refs[:n_x]
    (oa_ref, yf_ref, yb_ref, su_ref, gm_ref, d_ref, wg_ref, bg_ref, lg_ref, lb_ref,
     ws_ref, bs_ref, wo_ref, gate_ref, gf_ref, shf_ref, scf_ref, o_ref, h_ref) = refs[n_x:]
    y = d_ref[...] * su_ref[...] + yf_ref[...] + yb_ref[...]
    g1 = _gelu(y)
    o_ssm = g1 * _sigmoid(_dot(g1.astype(BF16), wg_ref[...]) + bg_ref[...])
    gw = gm_ref.shape[1] // 2
    u = _gelu(gm_ref[:, :gw])
    v = _gelu(gm_ref[:, gw:])
    mu = jnp.mean(v, axis=-1, keepdims=True)
    var = jnp.mean(jnp.square(v - mu), axis=-1, keepdims=True)
    v = ((v - mu) * lax.rsqrt(var + NORM_EPS) * lg_ref[...] + lb_ref[...]).astype(BF16)
    tm = u.shape[0]
    bs = bs_ref[...]
    rows = []
    for c in range(tm // GMLP_CHUNK):
        cols = []
        for g in range(gw // GMLP_GROUP_W):
            vb = v[c * GMLP_CHUNK:(c + 1) * GMLP_CHUNK, g * GMLP_GROUP_W:(g + 1) * GMLP_GROUP_W]
            cols.append(_dot(ws_ref[g], vb) + bs[:, g:g + 1])
        rows.append(jnp.concatenate(cols, axis=1))
    o_gmlp = u * jnp.concatenate(rows, axis=0)
    mix = jnp.concatenate([oa_ref[...], o_ssm.astype(BF16), o_gmlp.astype(BF16)], axis=1)
    x1 = _read_x(x_refs, n_ctx_tiles, off) + gate_ref[...] * _dot(mix, wo_ref[...])
    o_ref[...] = x1
    h_ref[...] = _norm_mod(x1, gf_ref[...], shf_ref[...], scf_ref[...]).astype(BF16)


def _mix_out(layer, xs, o_attn, y_f, y_b, su, gm, p, g_ffn, mod4, dims, skip_ctx):
    B, n_c, L, ntok = dims["B"], dims["n_c"], dims["L"], dims["ntok"]
    d = xs[0].shape[1]
    tm = n_c
    lt = L // tm
    attn_w, ssm_w, gm_w = o_attn.shape[1], su.shape[2], gm.shape[1]
    ngrp = p["w_s"].shape[1]
    off = B if skip_ctx else 0
    nrow = ntok - off * tm
    assert o_attn.shape[0] == nrow

    def row(i):
        return jnp.where(i + off < B, B, (i + off - B) // lt)

    def seq(i):
        i = i + off
        return (jnp.where(i < B, i, (i - B) // lt), jnp.where(i < B, 0, 1 + (i - B) % lt), 0)

    vec = lambda w: pl.BlockSpec((None, 1, w), lambda i: (layer, 0, 0))
    modspec = lambda k: pl.BlockSpec((None, None, 1, d), lambda i: (layer, row(i), 0, k))
    kern = functools.partial(_mix_out_kernel, n_x=len(xs), n_ctx_tiles=B, off=off)
    return pl.pallas_call(
        kern,
        grid=(nrow // tm,),
        in_specs=_x_specs(xs, tm, B, off) + [
            pl.BlockSpec((tm, attn_w), lambda i: (i, 0)),
            pl.BlockSpec((None, tm, ssm_w), seq),
            pl.BlockSpec((None, tm, ssm_w), seq),
            pl.BlockSpec((None, tm, ssm_w), seq),
            pl.BlockSpec((tm, gm_w), lambda i: (i + off, 0)),
            vec(ssm_w),
            _resident((None, ssm_w, ssm_w), lambda i: (layer, 0, 0)),
            vec(ssm_w), vec(gm_w // 2), vec(gm_w // 2),
            _resident((None, ngrp, GMLP_CHUNK, GMLP_CHUNK), lambda i: (layer, 0, 0, 0)),
            pl.BlockSpec((None, GMLP_CHUNK, ngrp), lambda i: (layer, 0, 0)),
            _resident((None, d, d), lambda i: (layer, 0, 0)),
            modspec(2), vec(d), modspec(3), modspec(4),
        ],
        out_specs=[pl.BlockSpec((tm, d), lambda i: (i, 0)), pl.BlockSpec((tm, d), lambda i: (i, 0))],
        out_shape=[jax.ShapeDtypeStruct((nrow, d), F32), jax.ShapeDtypeStruct((nrow, d), BF16)],
        compiler_params=_cparams(("arbitrary",)),
        name="mix_out",
    )(*xs, o_attn, y_f, y_b, su, gm, p["ssm_d"], p["w_glu"], p["b_glu"], p["ln_g"], p["ln_b"],
      p["w_s"], p["b_s_t"], p["w_out"], mod4, g_ffn, mod4, mod4)


def _ffn_kernel(hm_ref, hp_ref, hn_ref, x_ref, gate_ref, gfin_ref, wg_ref, wv_ref, cw_ref, cb_ref, wd_ref,
                o_ref, h_ref, mp_ref, mn_ref, *, n_ctx_tiles, n_c, L, final):
    i = pl.program_id(0)
    j = pl.program_id(1)
    tm = x_ref.shape[0]
    halo = hp_ref.shape[0]

    @pl.when(j == 0)
    def _():
        h_ref[0:halo, :] = hp_ref[...]
        h_ref[halo:halo + tm, :] = hm_ref[...]
        h_ref[halo + tm:, :] = hn_ref[...]
        is_ctx = i < n_ctx_tiles
        seq_len = jnp.where(is_ctx, n_c, L)
        start = lax.rem(jnp.where(is_ctx, i * tm, (i - n_ctx_tiles) * tm), seq_len)
        pos = start + lax.broadcasted_iota(jnp.int32, (tm, LANES), 0)
        first = pos == 0
        last = pos == seq_len - 1
        for k in range(1, tm // min(n_c, L) + 1):
            first = jnp.logical_or(first, pos == k * seq_len)
            last = jnp.logical_or(last, pos == (k + 1) * seq_len - 1)
        mp_ref[...] = jnp.where(first, 0.0, 1.0)
        mn_ref[...] = jnp.where(last, 0.0, 1.0)
        o_ref[...] = jnp.zeros_like(o_ref)

    tf = wg_ref.shape[1]
    rep = tf // LANES
    ge = _dot(h_ref[...], wg_ref[...])
    val = _dot(h_ref[halo:halo + tm, :], wv_ref[...])
    ext = tm + 2 * halo
    g_prev = pltpu.roll(ge, 1, axis=0)[halo:halo + tm]
    g_next = pltpu.roll(ge, ext - 1, axis=0)[halo:halo + tm]
    mp = jnp.concatenate([mp_ref[...]] * rep, axis=1)
    mn = jnp.concatenate([mn_ref[...]] * rep, axis=1)
    cw = cw_ref[...]
    gc = cw[0:1] * (g_prev * mp) + cw[1:2] * ge[halo:halo + tm] + cw[2:3] * (g_next * mn) + cb_ref[...]
    act = (gc * _sigmoid(gc) * val).astype(BF16)
    o_ref[...] += _dot(act, wd_ref[...])

    @pl.when(j == pl.num_programs(1) - 1)
    def _():
        xn = x_ref[...] + gate_ref[...] * o_ref[...]
        if final:
            xn = xn * lax.rsqrt(jnp.mean(xn * xn, axis=-1, keepdims=True) + NORM_EPS) * gfin_ref[...]
        o_ref[...] = xn


def _conv_ffn(layer, x1, h2, mod4, w_up, conv_w, conv_b, w_down, g_final, dims, skip_ctx, final):
    B, n_c, L = dims["B"], dims["n_c"], dims["L"]
    nrow, d = x1.shape
    d_ff = w_down.shape[1]
    tm, tf = dims["ffn_tm"], FFN_TF
    halo = BF16_ROWS
    nf = d_ff // tf
    nct = 0 if skip_ctx else B * n_c // tm
    lt = L // tm
    nhalo = nrow // halo

    def row(i):
        return jnp.where(i < nct, B, (i - nct) // lt)

    modspec = lambda k: pl.BlockSpec((None, None, 1, d), lambda i, j: (layer, row(i), 0, k))
    kern = functools.partial(_ffn_kernel, n_ctx_tiles=nct, n_c=n_c, L=L, final=final)
    return pl.pallas_call(
        kern,
        grid=(nrow // tm, nf),
        in_specs=[
            pl.BlockSpec((tm, d), lambda i, j: (i, 0)),
            pl.BlockSpec((halo, d), lambda i, j: (jnp.maximum(i * (tm // halo) - 1, 0), 0)),
            pl.BlockSpec((halo, d), lambda i, j: (jnp.minimum((i + 1) * (tm // halo), nhalo - 1), 0)),
            pl.BlockSpec((tm, d), lambda i, j: (i, 0)),
            modspec(5),
            pl.BlockSpec((1, d), lambda i, j: (0, 0)),
            pl.BlockSpec((None, d, tf), lambda i, j: (layer, 0, j)),
            pl.BlockSpec((None, d, tf), lambda i, j: (layer, 0, nf + j)),
            pl.BlockSpec((None, 3, tf), lambda i, j: (layer, 0, j)),
            pl.BlockSpec((None, 1, tf), lambda i, j: (layer, 0, j)),
            pl.BlockSpec((None, tf, d), lambda i, j: (layer, j, 0)),
        ],
        out_specs=pl.BlockSpec((tm, d), lambda i, j: (i, 0)),
        out_shape=jax.ShapeDtypeStruct(x1.shape, F32),
        scratch_shapes=[
            pltpu.VMEM((tm + 2 * halo, d), BF16),
            pltpu.VMEM((tm, LANES), F32),
            pltpu.VMEM((tm, LANES), F32),
        ],
        compiler_params=_cparams(("arbitrary", "arbitrary")),
        name="conv_ffn",
    )(h2, h2, h2, x1, mod4, g_final, w_up, w_up, conv_w, conv_b, w_down)


def _rope_tables(n_c, L):
    n_freq = HEAD_DIM // 4
    t = np.arange(L)
    inv_freq = jnp.asarray(ROPE_BASE, F32) ** (-jnp.arange(n_freq, dtype=F32) / n_freq)
    pos = jnp.asarray(np.stack([t // GRID_W, t % GRID_W], axis=-1), F32)
    ang = pos[:, :, None] * inv_freq
    cos, sin = jnp.cos(ang), jnp.sin(ang)
    c64 = jnp.concatenate([cos, cos], axis=-1).reshape(L, HEAD_DIM)
    s64 = jnp.concatenate([-sin, sin], axis=-1).reshape(L, HEAD_DIM)
    reps = LANES // HEAD_DIM
    c = jnp.concatenate([jnp.ones((n_c, LANES), F32), jnp.tile(c64, (1, reps))], axis=0)
    s = jnp.concatenate([jnp.zeros((n_c, LANES), F32), jnp.tile(s64, (1, reps))], axis=0)
    return c, s


def kernel(x, c, ctx, c_ctx, w_ada, b_ada, g_mix, g_ffn, w_in, w_out, attn_sink, ssm_lambda_re, ssm_lambda_im,
           ssm_log_dt, ssm_b_re, ssm_b_im, ssm_c_re, ssm_c_im, ssm_d, ssm_w_glu, ssm_b_glu, gmlp_ln_g, gmlp_ln_b,
           gmlp_w_s, gmlp_b_s, ffn_w_up, ffn_conv_w, ffn_conv_b, ffn_w_down, g_final):
    B, L, d = x.shape
    n_c = ctx.shape[1]
    depth = w_in.shape[0]
    ssm_w = ssm_d.shape[1]
    gmlp_w = gmlp_ln_g.shape[1]
    attn_w = w_in.shape[2] - ssm_w - 2 * gmlp_w
    attn_w = attn_w * GQA_RATIO // (GQA_RATIO + 2)
    kv_w = attn_w // GQA_RATIO
    ntok = B * (n_c + L)
    ffn_tm = min(FFN_TM, B * n_c)
    assert n_c % ATTN_BLOCK == 0 and L % n_c == 0 and L % ffn_tm == 0 and (B * n_c) % ffn_tm == 0
    assert n_c % S5_T == 0 and L % S5_T == 0 and B + 1 <= SUBLANES
    dims = dict(B=B, n_c=n_c, L=L, ntok=ntok, attn_w=attn_w, kv_w=kv_w, ssm_w=ssm_w, ffn_tm=ffn_tm)

    cpad = jnp.zeros((SUBLANES, d), F32).at[:B].set(c).at[B].set(c_ctx)
    mod = _modulation(cpad, w_ada, b_ada)
    mod4 = mod.reshape(depth, SUBLANES, 1, N_MOD * d)

    bm, cm, a_re, a_im = _s5_prepare(ssm_lambda_re, ssm_lambda_im, ssm_log_dt, ssm_b_re, ssm_b_im,
                                     ssm_c_re, ssm_c_im, B)
    cos_t, sin_t = _rope_tables(n_c, L)
    bias = _attn_bias(n_c)

    w_in_b = w_in.astype(BF16)
    w_up_b = ffn_w_up.astype(BF16)
    w_down_b = ffn_w_down.astype(BF16)
    mixp = dict(
        ssm_d=ssm_d.reshape(depth, 1, ssm_w), w_glu=ssm_w_glu.astype(BF16), b_glu=ssm_b_glu.reshape(depth, 1, ssm_w),
        ln_g=gmlp_ln_g.reshape(depth, 1, gmlp_w), ln_b=gmlp_ln_b.reshape(depth, 1, gmlp_w),
        w_s=gmlp_w_s.astype(BF16), b_s_t=jnp.swapaxes(gmlp_b_s, 1, 2), w_out=w_out.astype(BF16))
    g_mix3 = g_mix.reshape(depth, 1, d)
    g_ffn3 = g_ffn.reshape(depth, 1, d)
    conv_b3 = ffn_conv_b.reshape(depth, 1, -1)

    xs = (ctx.reshape(B * n_c, d), x.reshape(B * L, d))
    g_fin = g_final.reshape(1, d)
    for layer in range(depth):
        last = layer == depth - 1
        q, kx, vx, su, gm = _in_proj(layer, xs, g_mix3, mod4, w_in_b, cos_t, sin_t, dims)
        o_attn = _attention(layer, q, kx, vx, attn_sink, bias, dims, skip_ctx=last)
        y_f, y_b = _s5_scan(layer, su, bm, cm, a_re, a_im, dims)
        x1, h2 = _mix_out(layer, xs, o_attn, y_f, y_b, su, gm, mixp, g_ffn3, mod4, dims, skip_ctx=last)
        xs = (_conv_ffn(layer, x1, h2, mod4, w_up_b, ffn_conv_w, conv_b3, w_down_b, g_fin, dims,
                        skip_ctx=last, final=last),)
    return xs[0].reshape(B, L, d)
```

```python
import functools
import math

import numpy as np
import jax
import jax.numpy as jnp
from jax import lax
from jax.experimental import pallas as pl
from jax.experimental.pallas import tpu as pltpu

F32 = jnp.float32
BF16 = jnp.bfloat16

HEAD_DIM = 64
GQA_RATIO = 8
GRID_W = 64
ROPE_BASE = 10000.0
ATTN_BLOCK = 128
SSM_GROUP = 16
SSM_STATE = 64
GMLP_CHUNK = 128
GMLP_GROUP_W = 128
N_MOD = 6
NORM_EPS = 1e-6
MASK_NEG = -1e30
LOG2E = 1.4426950408889634

LANES = 128
SUBLANES = 8
BF16_ROWS = 16

MOD_TN = 1024
FFN_TM = 512
FFN_TF = 512
ATTN_STACK = 4
S5_T = 128
S5_PITCH = S5_T + 4
S5_SHIFT = 4
S5_PIECE = 256
VMEM_LIMIT = 56 * 1024 * 1024


def _cparams(sem):
    return pltpu.CompilerParams(dimension_semantics=sem, vmem_limit_bytes=VMEM_LIMIT)


def _sigmoid(x):
    return 1.0 / (1.0 + jnp.exp(-x))


def _gelu(x):
    return 0.5 * x * (1.0 + jnp.tanh(0.7978845608028654 * (x + 0.044715 * (x * x * x))))


def _norm_mod(x, g, shift, scale):
    y = x * lax.rsqrt(jnp.mean(x * x, axis=-1, keepdims=True) + NORM_EPS)
    return y * (g * (1.0 + scale)) + shift


def _dot(a, b):
    return jnp.dot(a, b, preferred_element_type=F32)


def _resident(block_shape, index_map):
    return pl.BlockSpec(block_shape, index_map, pipeline_mode=pl.Buffered(1))


def _mod_kernel(c_ref, w_ref, b_ref, o_ref):
    c = c_ref[...]
    act = (c * _sigmoid(c)).astype(BF16)
    o_ref[...] = _dot(act, w_ref[...].astype(BF16)) + b_ref[...]


def _modulation(cpad, w_ada, b_ada):
    depth, d, n = w_ada.shape
    rows = cpad.shape[0]
    return pl.pallas_call(
        _mod_kernel,
        grid=(depth, n // MOD_TN),
        in_specs=[
            pl.BlockSpec((rows, d), lambda l, j: (0, 0)),
            pl.BlockSpec((None, d, MOD_TN), lambda l, j: (l, 0, j)),
            pl.BlockSpec((None, 1, MOD_TN), lambda l, j: (l, 0, j)),
        ],
        out_specs=pl.BlockSpec((None, rows, MOD_TN), lambda l, j: (l, 0, j)),
        out_shape=jax.ShapeDtypeStruct((depth, rows, n), F32),
        compiler_params=_cparams(("arbitrary", "arbitrary")),
        name="adaln_mod",
    )(cpad, w_ada, b_ada.reshape(depth, 1, n))


def _rope(x, c, s, lane):
    first = (lane % 32) < 16
    partner = jnp.where(first, pltpu.roll(x, LANES - 16, axis=1), pltpu.roll(x, 16, axis=1))
    return x * c + partner * s


def _x_specs(xs, tm, n_ctx_tiles, off=0):
    d = xs[0].shape[1]
    if len(xs) == 1:
        return [pl.BlockSpec((tm, d), lambda i: (i + off, 0))]
    return [pl.BlockSpec((tm, d), lambda i: (jnp.minimum(i + off, n_ctx_tiles - 1), 0)),
            pl.BlockSpec((tm, d), lambda i: (jnp.maximum(i + off - n_ctx_tiles, 0), 0))]


def _read_x(x_refs, n_ctx_tiles, off=0):
    if len(x_refs) == 1:
        return x_refs[0][...]
    return jnp.where(pl.program_id(0) + off < n_ctx_tiles, x_refs[0][...], x_refs[1][...])


def _in_proj_kernel(*refs, attn_w, kv_w, ssm_w, n_x, n_ctx_tiles):
    x = _read_x(refs[:n_x], n_ctx_tiles)
    g_ref, sh_ref, sc_ref, w_ref, cos_ref, sin_ref, q_ref, kx_ref, vx_ref, su_ref, gm_ref = refs[n_x:]
    h = _norm_mod(x, g_ref[...], sh_ref[...], sc_ref[...]).astype(BF16)
    z = _dot(h, w_ref[...])
    tm = z.shape[0]
    c = cos_ref[...]
    s = sin_ref[...]
    lane = lax.broadcasted_iota(jnp.int32, (tm, LANES), 1)
    qscale = HEAD_DIM ** -0.5 * LOG2E
    for j in range(attn_w // LANES):
        zq = z[:, j * LANES:(j + 1) * LANES]
        q_ref[:, j * LANES:(j + 1) * LANES] = (_rope(zq, c, s, lane) * qscale).astype(BF16)
    low = lane < HEAD_DIM
    off_k, off_v = attn_w, attn_w + kv_w
    for src, dst, rot in ((z[:, off_k:off_k + kv_w], kx_ref, True), (z[:, off_v:off_v + kv_w], vx_ref, False)):
        t = _rope(src, c, s, lane) if rot else src
        h0 = jnp.where(low, t, 0.0)
        h1 = jnp.where(low, 0.0, t)
        parts = (h0, pltpu.roll(h0, HEAD_DIM, axis=1), pltpu.roll(h1, HEAD_DIM, axis=1), h1)
        for p, val in enumerate(parts):
            dst[:, p * LANES:(p + 1) * LANES] = val.astype(BF16)
    off_s = attn_w + 2 * kv_w
    su_ref[...] = z[:, off_s:off_s + ssm_w]
    gm_ref[...] = z[:, off_s + ssm_w:]


def _in_proj(layer, xs, g_mix, mod4, w_in, cos_t, sin_t, dims):
    B, n_c, L, ntok = dims["B"], dims["n_c"], dims["L"], dims["ntok"]
    d = xs[0].shape[1]
    n_in = w_in.shape[2]
    attn_w, kv_w, ssm_w = dims["attn_w"], dims["kv_w"], dims["ssm_w"]
    gm_w = n_in - attn_w - 2 * kv_w - ssm_w
    tm = n_c
    lt = L // tm
    S = n_c + L

    def row(i):
        return jnp.where(i < B, B, (i - B) // lt)

    def tbl(i):
        return jnp.where(i < B, 0, 1 + (i - B) % lt)

    def seq(i):
        return (jnp.where(i < B, i, (i - B) // lt), jnp.where(i < B, 0, 1 + (i - B) % lt), 0)

    kern = functools.partial(_in_proj_kernel, attn_w=attn_w, kv_w=kv_w, ssm_w=ssm_w, n_x=len(xs), n_ctx_tiles=B)
    return pl.pallas_call(
        kern,
        grid=(ntok // tm,),
        in_specs=_x_specs(xs, tm, B) + [
            pl.BlockSpec((None, 1, d), lambda i: (layer, 0, 0)),
            pl.BlockSpec((None, None, 1, d), lambda i: (layer, row(i), 0, 0)),
            pl.BlockSpec((None, None, 1, d), lambda i: (layer, row(i), 0, 1)),
            _resident((None, d, n_in), lambda i: (layer, 0, 0)),
            pl.BlockSpec((tm, LANES), lambda i: (tbl(i), 0)),
            pl.BlockSpec((tm, LANES), lambda i: (tbl(i), 0)),
        ],
        out_specs=[
            pl.BlockSpec((tm, attn_w), lambda i: (i, 0)),
            pl.BlockSpec((tm, 4 * LANES), lambda i: (i, 0)),
            pl.BlockSpec((tm, 4 * LANES), lambda i: (i, 0)),
            pl.BlockSpec((None, tm, ssm_w), seq),
            pl.BlockSpec((tm, gm_w), lambda i: (i, 0)),
        ],
        out_shape=[
            jax.ShapeDtypeStruct((ntok, attn_w), BF16),
            jax.ShapeDtypeStruct((ntok, 4 * LANES), BF16),
            jax.ShapeDtypeStruct((ntok, 4 * LANES), BF16),
            jax.ShapeDtypeStruct((B, S, ssm_w), F32),
            jax.ShapeDtypeStruct((ntok, gm_w), F32),
        ],
        compiler_params=_cparams(("arbitrary",)),
        name="in_proj",
    )(*xs, g_mix, mod4, mod4, w_in, cos_t, sin_t)


def _attn_kernel(sink_ref, q_ref, kp_ref, ko_ref, kn_ref, kc_ref, vp_ref, vo_ref, vn_ref, vc_ref,
                 bias_ref, o_ref, s_ref, p_ref, *, layer, n_kv):
    blk = q_ref.shape[0]
    per_kv = GQA_RATIO * HEAD_DIM // LANES
    bias = bias_ref[...]
    nkeys = bias.shape[1]
    low = lax.broadcasted_iota(jnp.int32, (blk, LANES), 1) < HEAD_DIM
    krefs = (kp_ref, ko_ref, kn_ref, kc_ref)
    vrefs = (vp_ref, vo_ref, vn_ref, vc_ref)
    for g in range(n_kv):
        def keys(refs, half):
            lo = (2 * g + half) * LANES
            return [r[:, lo:lo + LANES] for r in refs]
        kb = jnp.concatenate(keys(krefs, 0) + keys(krefs, 1), axis=0)
        vb = jnp.concatenate(keys(vrefs, 0) + keys(vrefs, 1), axis=0)
        for c0 in range(0, per_kv, ATTN_STACK):
            ch = g * (per_kv // ATTN_STACK) + c0 // ATTN_STACK
            cols = [(g * per_kv + c0 + c) * LANES for c in range(ATTN_STACK)]
            qs = jnp.concatenate([q_ref[:, col:col + LANES] for col in cols], axis=0)
            s_ref[ch] = lax.dot_general(qs, kb, (((1,), (1,)), ((), ())), preferred_element_type=F32)
            scales = []
            for c in range(ATTN_STACK):
                rows = slice(c * blk, (c + 1) * blk)
                invs = []
                for p in range(2):
                    sink = sink_ref[layer, g * GQA_RATIO + 2 * (c0 + c) + p] * LOG2E
                    slabs = [slice(p * nkeys + k * LANES, p * nkeys + (k + 1) * LANES) for k in range(nkeys // LANES)]
                    mx = None
                    for k, sl in enumerate(slabs):
                        sk = s_ref[ch, rows, sl] + bias[:, k * LANES:(k + 1) * LANES]
                        mx = sk if mx is None else jnp.maximum(mx, sk)
                    m = jnp.maximum(jnp.max(mx, axis=1, keepdims=True), sink)
                    acc = None
                    for k, sl in enumerate(slabs):
                        e = jnp.exp2(s_ref[ch, rows, sl] + bias[:, k * LANES:(k + 1) * LANES] - m)
                        p_ref[ch, rows, sl] = e.astype(BF16)
                        acc = e if acc is None else acc + e
                    den = jnp.sum(acc, axis=1, keepdims=True) + jnp.exp2(sink - m)
                    invs.append(1.0 / den)
                scales.append(jnp.where(low, invs[0], invs[1]))
            o = _dot(p_ref[ch], vb)
            for c, col in enumerate(cols):
                o_ref[:, col:col + LANES] = (o[c * blk:(c + 1) * blk] * scales[c]).astype(o_ref.dtype)


def _attn_bias(n_c):
    blk = ATTN_BLOCK
    r = np.arange(blk)[:, None]
    j = np.arange(blk)[None, :]
    zero = np.zeros((blk, blk), np.float32)
    neg = np.full((blk, blk), MASK_NEG, np.float32)
    prev = np.where(j >= r, 0.0, MASK_NEG).astype(np.float32)
    nxt = np.where(j <= r, 0.0, MASK_NEG).astype(np.float32)
    ctx = np.zeros((blk, n_c), np.float32)
    variants = [
        np.concatenate([neg, zero, nxt, ctx], 1),
        np.concatenate([prev, zero, nxt, ctx], 1),
        np.concatenate([prev, zero, neg, ctx], 1),
        np.concatenate([neg, neg, neg, ctx], 1),
    ]
    return jnp.asarray(np.stack(variants))


def _attention(layer, q, kx, vx, sink, bias, dims, skip_ctx):
    B, n_c, L, ntok = dims["B"], dims["n_c"], dims["L"], dims["ntok"]
    blk = ATTN_BLOCK
    attn_w = q.shape[1]
    n_kv = dims["kv_w"] // HEAD_DIM
    nblk = ntok // blk
    nctb = B * n_c // blk
    cpb = n_c // blk
    nb = L // blk
    assert nb >= 2
    off = nctb if skip_ctx else 0

    def bat(t):
        return jnp.where(t < nctb, t // cpb, (t - nctb) // nb)

    def variant(t):
        n = (t - nctb) % nb
        return jnp.where(t < nctb, 3, jnp.where(n == 0, 0, jnp.where(n == nb - 1, 2, 1)))

    kvw = kx.shape[1]
    own = pl.BlockSpec((blk, kvw), lambda t: (t + off, 0))
    prev = pl.BlockSpec((blk, kvw), lambda t: (jnp.maximum(t + off - 1, 0), 0))
    nxt = pl.BlockSpec((blk, kvw), lambda t: (jnp.minimum(t + off + 1, nblk - 1), 0))
    ctx = pl.BlockSpec((n_c, kvw), lambda t: (bat(t + off), 0))
    kern = functools.partial(_attn_kernel, layer=layer, n_kv=n_kv)
    return pl.pallas_call(
        kern,
        grid=(nblk - off,),
        in_specs=[
            pl.BlockSpec(memory_space=pltpu.SMEM),
            pl.BlockSpec((blk, attn_w), lambda t: (t + off, 0)),
            prev, own, nxt, ctx, prev, own, nxt, ctx,
            pl.BlockSpec((None,) + bias.shape[1:], lambda t: (variant(t + off), 0, 0)),
        ],
        out_specs=pl.BlockSpec((blk, attn_w), lambda t: (t, 0)),
        out_shape=jax.ShapeDtypeStruct(((nblk - off) * blk, attn_w), BF16),
        scratch_shapes=[
            pltpu.VMEM((attn_w // LANES // ATTN_STACK, ATTN_STACK * blk, 2 * bias.shape[2]), F32),
            pltpu.VMEM((attn_w // LANES // ATTN_STACK, ATTN_STACK * blk, 2 * bias.shape[2]), BF16),
        ],
        compiler_params=_cparams(("arbitrary",)),
        name="window_attn",
    )(sink, q, kx, kx, kx, kx, vx, vx, vx, vx, bias)


def _s5_prep_kernel(lre_ref, lim_ref, ldt_ref, bre_ref, bim_ref, are_ref, aim_ref, bbre_ref, bbim_ref):
    lre = lre_ref[...]
    lim = lim_ref[...]
    dt = jnp.exp(ldt_ref[...])
    mag = jnp.exp(lre * dt)
    a_re = mag * jnp.cos(lim * dt)
    a_im = mag * jnp.sin(lim * dt)
    den = lre * lre + lim * lim
    n_re = a_re - 1.0
    f_re = ((n_re * lre + a_im * lim) / den)[:, None, :]
    f_im = ((a_im * lre - n_re * lim) / den)[:, None, :]
    are_ref[...] = a_re
    aim_ref[...] = a_im
    bre = bre_ref[...]
    bim = bim_ref[...]
    bbre_ref[...] = f_re * bre - f_im * bim
    bbim_ref[...] = f_re * bim + f_im * bre


def _s5_prepare(lam_re, lam_im, log_dt, b_re, b_im, c_re, c_im, B):
    depth, _, G, P = lam_re.shape
    H = b_re.shape[-1]
    n = depth * 2 * G
    bT_re = jnp.swapaxes(b_re, -1, -2).reshape(n, H, P)
    bT_im = jnp.swapaxes(b_im, -1, -2).reshape(n, H, P)
    a_re, a_im, bb_re, bb_im = pl.pallas_call(
        _s5_prep_kernel,
        out_shape=[jax.ShapeDtypeStruct((n, P), F32), jax.ShapeDtypeStruct((n, P), F32),
                   jax.ShapeDtypeStruct((n, H, P), F32), jax.ShapeDtypeStruct((n, H, P), F32)],
        name="s5_discretize",
    )(lam_re.reshape(n, P), lam_im.reshape(n, P), log_dt.reshape(n, 1), bT_re, bT_im)
    gh = G // 2
    gq = S5_PIECE // P
    nq = gh // gq
    bb = jnp.stack([bb_re, bb_im], axis=2).astype(BF16).reshape(depth, 2, 2, gh, H, 2, P)
    bb = jnp.transpose(bb, (0, 1, 2, 5, 3, 4, 6))[:, :, :, :, None, :, :, None, :]
    g_row = jnp.arange(gh).reshape(1, gh, 1, 1, 1)
    g_col = (gq * jnp.arange(nq).reshape(nq, 1, 1, 1, 1) + jnp.arange(gq).reshape(1, 1, 1, gq, 1))
    bm = jnp.where(g_row == g_col, jnp.broadcast_to(bb, (depth, 2, 2, 2, nq, gh, H, gq, P)), 0)
    bm = bm.reshape(depth, 2, 2, 2 * nq, gh * H, gq * P)
    cc = jnp.stack([c_re, -c_im], axis=4).astype(BF16).reshape(depth, 2, 2, nq, gq, H, 2, P)
    cc = jnp.transpose(cc, (0, 1, 2, 6, 3, 4, 7, 5))[:, :, :, :, :, :, :, None, :]
    g_own = (gq * jnp.arange(nq).reshape(nq, 1, 1, 1, 1) + jnp.arange(gq).reshape(1, gq, 1, 1, 1))
    g_out = jnp.arange(gh).reshape(1, 1, 1, gh, 1)
    cm = jnp.where(g_own == g_out, jnp.broadcast_to(cc, (depth, 2, 2, 2, nq, gq, P, gh, H)), 0)
    cm = cm.reshape(depth, 2, 2, 2 * nq, gq * P, gh * H)
    a_re = jnp.tile(a_re.reshape(depth, 2, 2, gh * P), (1, 1, B, 1))
    a_im = jnp.tile(a_im.reshape(depth, 2, 2, gh * P), (1, 1, B, 1))
    return bm, cm, a_re, a_im


def _s5_kernel(uf0_ref, ufn_ref, ub_ref, bm_ref, cm_ref, are_ref, aim_ref, yf_ref, yb_ref,
               buf_f, buf_b, h_ref, yacc_ref, lhs_ref, *, B, n_chunks):
    T, pitch, shift = S5_T, S5_PITCH, S5_SHIFT
    R = 2 * B
    wide = T + SUBLANES
    hw = bm_ref.shape[3]
    npiece = bm_ref.shape[2]
    sw = are_ref.shape[2]
    nsl = sw // LANES
    per = S5_PIECE // LANES
    steps = T // (2 * npiece)
    s = pl.program_id(0)

    def window(b, k):
        r = 2 * b + k
        return (r * pitch - shift, wide) if k else (r * pitch, T)

    def put_lhs(u_ref):
        for k in range(2):
            parts = []
            for b in range(B):
                ub = u_ref[b, :, k * hw:(k + 1) * hw]
                if k:
                    ub = pltpu.roll(jnp.concatenate([ub, jnp.zeros((SUBLANES, hw), F32)], axis=0), shift, axis=0)
                parts.append(ub)
            rows = B * (wide if k else T)
            lhs_ref[k, 0:rows, :] = jnp.concatenate(parts, axis=0).astype(BF16)

    def bu_piece(buf, d, k, q):
        rows = wide if k else T
        bu = _dot(lhs_ref[k, 0:B * rows, :], bm_ref[d, k, q])
        for b in range(B):
            base, _ = window(b, k)
            for e in range(per):
                buf[per * q + e, base:base + rows, :] = bu[b * rows:(b + 1) * rows, e * LANES:(e + 1) * LANES]

    def readout_piece(buf, d, k, q):
        rows = wide if k else T
        parts = []
        for b in range(B):
            base, _ = window(b, k)
            parts.append(jnp.concatenate([buf[per * q + e, base:base + rows, :] for e in range(per)], axis=1))
        yacc_ref[k, 0:B * rows, :] += _dot(jnp.concatenate(parts, axis=0).astype(BF16), cm_ref[d, k, q])

    def scan_step(buf, d, t, h_re, h_im):
        idx = pl.ds(t, R, stride=pitch)
        b_re = jnp.concatenate([buf[sl, idx, :] for sl in range(nsl)], axis=1)
        b_im = jnp.concatenate([buf[nsl + sl, idx, :] for sl in range(nsl)], axis=1)
        a_re = are_ref[d]
        a_im = aim_ref[d]
        n_re = a_re * h_re - a_im * h_im + b_re
        n_im = a_re * h_im + a_im * h_re + b_im
        for sl in range(nsl):
            buf[sl, idx, :] = n_re[:, sl * LANES:(sl + 1) * LANES]
            buf[nsl + sl, idx, :] = n_im[:, sl * LANES:(sl + 1) * LANES]
        return n_re, n_im

    def slot(d_scan, buf_scan, d_mx, buf_mx, u_mx_ref, y_mx_ref):
        put_lhs(u_mx_ref)
        yacc_ref[...] = jnp.zeros_like(yacc_ref)
        carry = (h_ref[d_scan, 0], h_ref[d_scan, 1])
        for k in range(2):
            h_re, h_im = carry
            for q in range(npiece):
                readout_piece(buf_mx, d_mx, k, q)
                bu_piece(buf_mx, d_mx, k, q)
                for i in range(steps):
                    t = (k * npiece + q) * steps + i
                    h_re, h_im = scan_step(buf_scan, d_scan, T - 1 - t if d_scan else t, h_re, h_im)
            carry = (h_re, h_im)
            rows = wide if k else T
            for b in range(B):
                yb = yacc_ref[k, b * rows:(b + 1) * rows, :]
                if k:
                    yb = pltpu.roll(yb, rows - shift, axis=0)[:T]
                y_mx_ref[b, :, k * hw:(k + 1) * hw] = yb
        h_ref[d_scan, 0] = carry[0]
        h_ref[d_scan, 1] = carry[1]

    @pl.when(s == 0)
    def _():
        h_ref[...] = jnp.zeros_like(h_ref)
        buf_b[...] = jnp.zeros_like(buf_b)
        put_lhs(uf0_ref)
        for k in range(2):
            for q in range(npiece):
                bu_piece(buf_f, 0, k, q)

    slot(0, buf_f, 1, buf_b, ub_ref, yb_ref)

    @pl.when(s < n_chunks)
    def _():
        slot(1, buf_b, 0, buf_f, ufn_ref, yf_ref)


def _s5_scan(layer, su, bm, cm, a_re, a_im, dims):
    B, n_c, L = dims["B"], dims["n_c"], dims["L"]
    T = S5_T
    S = n_c + L
    ssm_w = su.shape[2]
    ncb, nlb = n_c // T, L // T
    R = 2 * B
    sw = a_re.shape[3]
    nslab = 2 * sw // LANES

    def bwd(s):
        return jnp.where(s < ncb, ncb - 1 - s, ncb + (nlb - 1 - (s - ncb)))

    blk = (B, T, ssm_w)
    nch = S // T
    last = nch - 1
    wide_rows = B * (T + SUBLANES)
    kern = functools.partial(_s5_kernel, B=B, n_chunks=nch)
    return pl.pallas_call(
        kern,
        grid=(nch + 1,),
        in_specs=[
            pl.BlockSpec(blk, lambda s: (0, 0, 0)),
            pl.BlockSpec(blk, lambda s: (0, jnp.minimum(s + 1, last), 0)),
            pl.BlockSpec(blk, lambda s: (0, bwd(jnp.minimum(s, last)), 0)),
            _resident((None,) + bm.shape[1:], lambda s: (layer, 0, 0, 0, 0, 0)),
            _resident((None,) + cm.shape[1:], lambda s: (layer, 0, 0, 0, 0, 0)),
            _resident((None,) + a_re.shape[1:], lambda s: (layer, 0, 0, 0)),
            _resident((None,) + a_im.shape[1:], lambda s: (layer, 0, 0, 0)),
        ],
        out_specs=[pl.BlockSpec(blk, lambda s: (0, jnp.minimum(s, last), 0)),
                   pl.BlockSpec(blk, lambda s: (0, bwd(jnp.maximum(s - 1, 0)), 0))],
        out_shape=[jax.ShapeDtypeStruct(su.shape, F32), jax.ShapeDtypeStruct(su.shape, F32)],
        scratch_shapes=[
            pltpu.VMEM((nslab, R * S5_PITCH, LANES), F32),
            pltpu.VMEM((nslab, R * S5_PITCH, LANES), F32),
            pltpu.VMEM((2, 2, R, sw), F32),
            pltpu.VMEM((2, wide_rows, S5_PIECE), F32),
            pltpu.VMEM((2, wide_rows, ssm_w // 2), BF16),
        ],
        compiler_params=_cparams(("arbitrary",)),
        name="s5_scan",
    )(su, su, su, bm, cm, a_re, a_im)


def _mix_out_kernel(*refs, n_x, n_ctx_tiles, off):
    x_refs = refs[:n_x]
    (oa_ref, yf_ref, yb_ref, su_ref, gm_ref, d_ref, wg_ref, bg_ref, lg_ref, lb_ref,
     ws_ref, bs_ref, wo_ref, gate_ref, gf_ref, shf_ref, scf_ref, o_ref, h_ref) = refs[n_x:]
    y = d_ref[...] * su_ref[...] + yf_ref[...] + yb_ref[...]
    g1 = _gelu(y)
    o_ssm = g1 * _sigmoid(_dot(g1.astype(BF16), wg_ref[...]) + bg_ref[...])
    gw = gm_ref.shape[1] // 2
    u = _gelu(gm_ref[:, :gw])
    v = _gelu(gm_ref[:, gw:])
    mu = jnp.mean(v, axis=-1, keepdims=True)
    var = jnp.mean(jnp.square(v - mu), axis=-1, keepdims=True)
    v = ((v - mu) * lax.rsqrt(var + NORM_EPS) * lg_ref[...] + lb_ref[...]).astype(BF16)
    tm = u.shape[0]
    bs = bs_ref[...]
    rows = []
    for c in range(tm // GMLP_CHUNK):
        cols = []
        for g in range(gw // GMLP_GROUP_W):
            vb = v[c * GMLP_CHUNK:(c + 1) * GMLP_CHUNK, g * GMLP_GROUP_W:(g + 1) * GMLP_GROUP_W]
            cols.append(_dot(ws_ref[g], vb) + bs[:, g:g + 1])
        rows.append(jnp.concatenate(cols, axis=1))
    o_gmlp = u * jnp.concatenate(rows, axis=0)
    mix = jnp.concatenate([oa_ref[...], o_ssm.astype(BF16), o_gmlp.astype(BF16)], axis=1)
    x1 = _read_x(x_refs, n_ctx_tiles, off) + gate_ref[...] * _dot(mix, wo_ref[...])
    o_ref[...] = x1
    h_ref[...] = _norm_mod(x1, gf_ref[...], shf_ref[...], scf_ref[...]).astype(BF16)


def _mix_out(layer, xs, o_attn, y_f, y_b, su, gm, p, g_ffn, mod4, dims, skip_ctx):
    B, n_c, L, ntok = dims["B"], dims["n_c"], dims["L"], dims["ntok"]
    d = xs[0].shape[1]
    tm = n_c
    lt = L // tm
    attn_w, ssm_w, gm_w = o_attn.shape[1], su.shape[2], gm.shape[1]
    ngrp = p["w_s"].shape[1]
    off = B if skip_ctx else 0
    nrow = ntok - off * tm
    assert o_attn.shape[0] == nrow

    def row(i):
        return jnp.where(i + off < B, B, (i + off - B) // lt)

    def seq(i):
        i = i + off
        return (jnp.where(i < B, i, (i - B) // lt), jnp.where(i < B, 0, 1 + (i - B) % lt), 0)

    vec = lambda w: pl.BlockSpec((None, 1, w), lambda i: (layer, 0, 0))
    modspec = lambda k: pl.BlockSpec((None, None, 1, d), lambda i: (layer, row(i), 0, k))
    kern = functools.partial(_mix_out_kernel, n_x=len(xs), n_ctx_tiles=B, off=off)
    return pl.pallas_call(
        kern,
        grid=(nrow // tm,),
        in_specs=_x_specs(xs, tm, B, off) + [
            pl.BlockSpec((tm, attn_w), lambda i: (i, 0)),
            pl.BlockSpec((None, tm, ssm_w), seq),
            pl.BlockSpec((None, tm, ssm_w), seq),
            pl.BlockSpec((None, tm, ssm_w), seq),
            pl.BlockSpec((tm, gm_w), lambda i: (i + off, 0)),
            vec(ssm_w),
            _resident((None, ssm_w, ssm_w), lambda i: (layer, 0, 0)),
            vec(ssm_w), vec(gm_w // 2), vec(gm_w // 2),
            _resident((None, ngrp, GMLP_CHUNK, GMLP_CHUNK), lambda i: (layer, 0, 0, 0)),
            pl.BlockSpec((None, GMLP_CHUNK, ngrp), lambda i: (layer, 0, 0)),
            _resident((None, d, d), lambda i: (layer, 0, 0)),
            modspec(2), vec(d), modspec(3), modspec(4),
        ],
        out_specs=[pl.BlockSpec((tm, d), lambda i: (i, 0)), pl.BlockSpec((tm, d), lambda i: (i, 0))],
        out_shape=[jax.ShapeDtypeStruct((nrow, d), F32), jax.ShapeDtypeStruct((nrow, d), BF16)],
        compiler_params=_cparams(("arbitrary",)),
        name="mix_out",
    )(*xs, o_attn, y_f, y_b, su, gm, p["ssm_d"], p["w_glu"], p["b_glu"], p["ln_g"], p["ln_b"],
      p["w_s"], p["b_s_t"], p["w_out"], mod4, g_ffn, mod4, mod4)


def _ffn_kernel(hm_ref, hp_ref, hn_ref, x_ref, gate_ref, gfin_ref, wg_ref, wv_ref, cw_ref, cb_ref, wd_ref,
                o_ref, h_ref, mp_ref, mn_ref, *, n_ctx_tiles, n_c, L, final):
    i = pl.program_id(0)
    j = pl.program_id(1)
    tm = x_ref.shape[0]
    halo = hp_ref.shape[0]

    @pl.when(j == 0)
    def _():
        h_ref[0:halo, :] = hp_ref[...]
        h_ref[halo:halo + tm, :] = hm_ref[...]
        h_ref[halo + tm:, :] = hn_ref[...]
        is_ctx = i < n_ctx_tiles
        seq_len = jnp.where(is_ctx, n_c, L)
        start = lax.rem(jnp.where(is_ctx, i * tm, (i - n_ctx_tiles) * tm), seq_len)
        pos = start + lax.broadcasted_iota(jnp.int32, (tm, LANES), 0)
        first = pos == 0
        last = pos == seq_len - 1
        for k in range(1, tm // min(n_c, L) + 1):
            first = jnp.logical_or(first, pos == k * seq_len)
            last = jnp.logical_or(last, pos == (k + 1) * seq_len - 1)
        mp_ref[...] = jnp.where(first, 0.0, 1.0)
        mn_ref[...] = jnp.where(last, 0.0, 1.0)
        o_ref[...] = jnp.zeros_like(o_ref)

    tf = wg_ref.shape[1]
    rep = tf // LANES
    ge = _dot(h_ref[...], wg_ref[...])
    val = _dot(h_ref[halo:halo + tm, :], wv_ref[...])
    ext = tm + 2 * halo
    g_prev = pltpu.roll(ge, 1, axis=0)[halo:halo + tm]
    g_next = pltpu.roll(ge, ext - 1, axis=0)[halo:halo + tm]
    mp = jnp.concatenate([mp_ref[...]] * rep, axis=1)
    mn = jnp.concatenate([mn_ref[...]] * rep, axis=1)
    cw = cw_ref[...]
    gc = cw[0:1] * (g_prev * mp) + cw[1:2] * ge[halo:halo + tm] + cw[2:3] * (g_next * mn) + cb_ref[...]
    act = (gc * _sigmoid(gc) * val).astype(BF16)
    o_ref[...] += _dot(act, wd_ref[...])

    @pl.when(j == pl.num_programs(1) - 1)
    def _():
        xn = x_ref[...] + gate_ref[...] * o_ref[...]
        if final:
            xn = xn * lax.rsqrt(jnp.mean(xn * xn, axis=-1, keepdims=True) + NORM_EPS) * gfin_ref[...]
        o_ref[...] = xn


def _conv_ffn(layer, x1, h2, mod4, w_up, conv_w, conv_b, w_down, g_final, dims, skip_ctx, final):
    B, n_c, L = dims["B"], dims["n_c"], dims["L"]
    nrow, d = x1.shape
    d_ff = w_down.shape[1]
    tm, tf = dims["ffn_tm"], FFN_TF
    halo = BF16_ROWS
    nf = d_ff // tf
    nct = 0 if skip_ctx else B * n_c // tm
    lt = L // tm
    nhalo = nrow // halo

    def row(i):
        return jnp.where(i < nct, B, (i - nct) // lt)

    modspec = lambda k: pl.BlockSpec((None, None, 1, d), lambda i, j: (layer, row(i), 0, k))
    kern = functools.partial(_ffn_kernel, n_ctx_tiles=nct, n_c=n_c, L=L, final=final)
    return pl.pallas_call(
        kern,
        grid=(nrow // tm, nf),
        in_specs=[
            pl.BlockSpec((tm, d), lambda i, j: (i, 0)),
            pl.BlockSpec((halo, d), lambda i, j: (jnp.maximum(i * (tm // halo) - 1, 0), 0)),
            pl.BlockSpec((halo, d), lambda i, j: (jnp.minimum((i + 1) * (tm // halo), nhalo - 1), 0)),
            pl.BlockSpec((tm, d), lambda i, j: (i, 0)),
            modspec(5),
            pl.BlockSpec((1, d), lambda i, j: (0, 0)),
            pl.BlockSpec((None, d, tf), lambda i, j: (layer, 0, j)),
            pl.BlockSpec((None, d, tf), lambda i, j: (layer, 0, nf + j)),
            pl.BlockSpec((None, 3, tf), lambda i, j: (layer, 0, j)),
            pl.BlockSpec((None, 1, tf), lambda i, j: (layer, 0, j)),
            pl.BlockSpec((None, tf, d), lambda i, j: (layer, j, 0)),
        ],
        out_specs=pl.BlockSpec((tm, d), lambda i, j: (i, 0)),
        out_shape=jax.ShapeDtypeStruct(x1.shape, F32),
        scratch_shapes=[
            pltpu.VMEM((tm + 2 * halo, d), BF16),
            pltpu.VMEM((tm, LANES), F32),
            pltpu.VMEM((tm, LANES), F32),
        ],
        compiler_params=_cparams(("arbitrary", "arbitrary")),
        name="conv_ffn",
    )(h2, h2, h2, x1, mod4, g_final, w_up, w_up, conv_w, conv_b, w_down)


def _rope_tables(n_c, L):
    n_freq = HEAD_DIM // 4
    t = np.arange(L)
    inv_freq = jnp.asarray(ROPE_BASE, F32) ** (-jnp.arange(n_freq, dtype=F32) / n_freq)
    pos = jnp.asarray(np.stack([t // GRID_W, t % GRID_W], axis=-1), F32)
    ang = pos[:, :, None] * inv_freq
    cos, sin = jnp.cos(ang), jnp.sin(ang)
    c64 = jnp.concatenate([cos, cos], axis=-1).reshape(L, HEAD_DIM)
    s64 = jnp.concatenate([-sin, sin], axis=-1).reshape(L, HEAD_DIM)
    reps = LANES // HEAD_DIM
    c = jnp.concatenate([jnp.ones((n_c, LANES), F32), jnp.tile(c64, (1, reps))], axis=0)
    s = jnp.concatenate([jnp.zeros((n_c, LANES), F32), jnp.tile(s64, (1, reps))], axis=0)
    return c, s


def kernel(x, c, ctx, c_ctx, w_ada, b_ada, g_mix, g_ffn, w_in, w_out, attn_sink, ssm_lambda_re, ssm_lambda_im,
           ssm_log_dt, ssm_b_re, ssm_b_im, ssm_c_re, ssm_c_im, ssm_d, ssm_w_glu, ssm_b_glu, gmlp_ln_g, gmlp_ln_b,
           gmlp_w_s, gmlp_b_s, ffn_w_up, ffn_conv_w, ffn_conv_b, ffn_w_down, g_final):
    B, L, d = x.shape
    n_c = ctx.shape[1]
    depth = w_in.shape[0]
    ssm_w = ssm_d.shape[1]
    gmlp_w = gmlp_ln_g.shape[1]
    attn_w = w_in.shape[2] - ssm_w - 2 * gmlp_w
    attn_w = attn_w * GQA_RATIO // (GQA_RATIO + 2)
    kv_w = attn_w // GQA_RATIO
    ntok = B * (n_c + L)
    ffn_tm = min(FFN_TM, B * n_c)
    assert n_c % ATTN_BLOCK == 0 and L % n_c == 0 and L % ffn_tm == 0 and (B * n_c) % ffn_tm == 0
    assert n_c % S5_T == 0 and L % S5_T == 0 and B + 1 <= SUBLANES
    dims = dict(B=B, n_c=n_c, L=L, ntok=ntok, attn_w=attn_w, kv_w=kv_w, ssm_w=ssm_w, ffn_tm=ffn_tm)

    cpad = jnp.zeros((SUBLANES, d), F32).at[:B].set(c).at[B].set(c_ctx)
    mod = _modulation(cpad, w_ada, b_ada)
    mod4 = mod.reshape(depth, SUBLANES, 1, N_MOD * d)

    bm, cm, a_re, a_im = _s5_prepare(ssm_lambda_re, ssm_lambda_im, ssm_log_dt, ssm_b_re, ssm_b_im,
                                     ssm_c_re, ssm_c_im, B)
    cos_t, sin_t = _rope_tables(n_c, L)
    bias = _attn_bias(n_c)

    w_in_b = w_in.astype(BF16)
    w_up_b = ffn_w_up.astype(BF16)
    w_down_b = ffn_w_down.astype(BF16)
    mixp = dict(
        ssm_d=ssm_d.reshape(depth, 1, ssm_w), w_glu=ssm_w_glu.astype(BF16), b_glu=ssm_b_glu.reshape(depth, 1, ssm_w),
        ln_g=gmlp_ln_g.reshape(depth, 1, gmlp_w), ln_b=gmlp_ln_b.reshape(depth, 1, gmlp_w),
        w_s=gmlp_w_s.astype(BF16), b_s_t=jnp.swapaxes(gmlp_b_s, 1, 2), w_out=w_out.astype(BF16))
    g_mix3 = g_mix.reshape(depth, 1, d)
    g_ffn3 = g_ffn.reshape(depth, 1, d)
    conv_b3 = ffn_conv_b.reshape(depth, 1, -1)

    xs = (ctx.reshape(B * n_c, d), x.reshape(B * L, d))
    g_fin = g_final.reshape(1, d)
    for layer in range(depth):
        last = layer == depth - 1
        q, kx, vx, su, gm = _in_proj(layer, xs, g_mix3, mod4, w_in_b, cos_t, sin_t, dims)
        o_attn = _attention(layer, q, kx, vx, attn_sink, bias, dims, skip_ctx=last)
        y_f, y_b = _s5_scan(layer, su, bm, cm, a_re, a_im, dims)
        x1, h2 = _mix_out(layer, xs, o_attn, y_f, y_b, su, gm, mixp, g_ffn3, mod4, dims, skip_ctx=last)
        xs = (_conv_ffn(layer, x1, h2, mod4, w_up_b, ffn_conv_w, conv_b3, w_down_b, g_fin, dims,
                        skip_ctx=last, final=last),)
    return xs[0].reshape(B, L, d)
```

```python
import functools
import math

import numpy as np
import jax
import jax.numpy as jnp
from jax import lax
from jax.experimental import pallas as pl
from jax.experimental.pallas import tpu as pltpu

F32 = jnp.float32
BF16 = jnp.bfloat16

HEAD_DIM = 64
GQA_RATIO = 8
GRID_W = 64
ROPE_BASE = 10000.0
ATTN_BLOCK = 128
SSM_GROUP = 16
SSM_STATE = 64
GMLP_CHUNK = 128
GMLP_GROUP_W = 128
N_MOD = 6
NORM_EPS = 1e-6
MASK_NEG = -1e30
LOG2E = 1.4426950408889634

LANES = 128
SUBLANES = 8
BF16_ROWS = 16

MOD_TN = 1024
FFN_TM = 512
FFN_TF = 512
ATTN_STACK = 2
S5_T = 128
S5_PITCH = S5_T + 4
S5_SHIFT = 4
S5_PIECE = 256
VMEM_LIMIT = 56 * 1024 * 1024


def _cparams(sem):
    return pltpu.CompilerParams(dimension_semantics=sem, vmem_limit_bytes=VMEM_LIMIT)


def _sigmoid(x):
    return 1.0 / (1.0 + jnp.exp(-x))


def _gelu(x):
    return 0.5 * x * (1.0 + jnp.tanh(0.7978845608028654 * (x + 0.044715 * (x * x * x))))


def _norm_mod(x, g, shift, scale):
    y = x * lax.rsqrt(jnp.mean(x * x, axis=-1, keepdims=True) + NORM_EPS)
    return y * (g * (1.0 + scale)) + shift


def _dot(a, b):
    return jnp.dot(a, b, preferred_element_type=F32)


def _resident(block_shape, index_map):
    return pl.BlockSpec(block_shape, index_map, pipeline_mode=pl.Buffered(1))


def _mod_kernel(c_ref, w_ref, b_ref, o_ref):
    c = c_ref[...]
    act = (c * _sigmoid(c)).astype(BF16)
    o_ref[...] = _dot(act, w_ref[...].astype(BF16)) + b_ref[...]


def _modulation(cpad, w_ada, b_ada):
    depth, d, n = w_ada.shape
    rows = cpad.shape[0]
    return pl.pallas_call(
        _mod_kernel,
        grid=(depth, n // MOD_TN),
        in_specs=[
            pl.BlockSpec((rows, d), lambda l, j: (0, 0)),
            pl.BlockSpec((None, d, MOD_TN), lambda l, j: (l, 0, j)),
            pl.BlockSpec((None, 1, MOD_TN), lambda l, j: (l, 0, j)),
        ],
        out_specs=pl.BlockSpec((None, rows, MOD_TN), lambda l, j: (l, 0, j)),
        out_shape=jax.ShapeDtypeStruct((depth, rows, n), F32),
        compiler_params=_cparams(("arbitrary", "arbitrary")),
        name="adaln_mod",
    )(cpad, w_ada, b_ada.reshape(depth, 1, n))


def _rope(x, c, s, lane):
    first = (lane % 32) < 16
    partner = jnp.where(first, pltpu.roll(x, LANES - 16, axis=1), pltpu.roll(x, 16, axis=1))
    return x * c + partner * s


def _x_specs(xs, tm, n_ctx_tiles, off=0):
    d = xs[0].shape[1]
    if len(xs) == 1:
        return [pl.BlockSpec((tm, d), lambda i: (i + off, 0))]
    return [pl.BlockSpec((tm, d), lambda i: (jnp.minimum(i + off, n_ctx_tiles - 1), 0)),
            pl.BlockSpec((tm, d), lambda i: (jnp.maximum(i + off - n_ctx_tiles, 0), 0))]


def _read_x(x_refs, n_ctx_tiles, off=0, rows=slice(None)):
    if len(x_refs) == 1:
        return x_refs[0][rows, :]
    return jnp.where(pl.program_id(0) + off < n_ctx_tiles, x_refs[0][rows, :], x_refs[1][rows, :])


def _in_proj_kernel(*refs, attn_w, kv_w, ssm_w, n_x, n_ctx_tiles):
    x = _read_x(refs[:n_x], n_ctx_tiles)
    g_ref, sh_ref, sc_ref, w_ref, cos_ref, sin_ref, q_ref, kx_ref, vx_ref, su_ref, gm_ref = refs[n_x:]
    h = _norm_mod(x, g_ref[...], sh_ref[...], sc_ref[...]).astype(BF16)
    z = _dot(h, w_ref[...])
    tm = z.shape[0]
    c = cos_ref[...]
    s = sin_ref[...]
    lane = lax.broadcasted_iota(jnp.int32, (tm, LANES), 1)
    qscale = HEAD_DIM ** -0.5 * LOG2E
    for j in range(attn_w // LANES):
        zq = z[:, j * LANES:(j + 1) * LANES]
        q_ref[:, j * LANES:(j + 1) * LANES] = (_rope(zq, c, s, lane) * qscale).astype(BF16)
    low = lane < HEAD_DIM
    off_k, off_v = attn_w, attn_w + kv_w
    for src, dst, rot in ((z[:, off_k:off_k + kv_w], kx_ref, True), (z[:, off_v:off_v + kv_w], vx_ref, False)):
        t = _rope(src, c, s, lane) if rot else src
        h0 = jnp.where(low, t, 0.0)
        h1 = jnp.where(low, 0.0, t)
        parts = (h0, pltpu.roll(h0, HEAD_DIM, axis=1), pltpu.roll(h1, HEAD_DIM, axis=1), h1)
        for p, val in enumerate(parts):
            dst[:, p * LANES:(p + 1) * LANES] = val.astype(BF16)
    off_s = attn_w + 2 * kv_w
    su_ref[...] = z[:, off_s:off_s + ssm_w]
    gm_ref[...] = z[:, off_s + ssm_w:]


def _in_proj(layer, xs, g_mix, mod4, w_in, cos_t, sin_t, dims):
    B, n_c, L, ntok = dims["B"], dims["n_c"], dims["L"], dims["ntok"]
    d = xs[0].shape[1]
    n_in = w_in.shape[2]
    attn_w, kv_w, ssm_w = dims["attn_w"], dims["kv_w"], dims["ssm_w"]
    gm_w = n_in - attn_w - 2 * kv_w - ssm_w
    tm = n_c
    lt = L // tm
    S = n_c + L

    def row(i):
        return jnp.where(i < B, B, (i - B) // lt)

    def tbl(i):
        return jnp.where(i < B, 0, 1 + (i - B) % lt)

    def seq(i):
        return (jnp.where(i < B, i, (i - B) // lt), jnp.where(i < B, 0, 1 + (i - B) % lt), 0)

    kern = functools.partial(_in_proj_kernel, attn_w=attn_w, kv_w=kv_w, ssm_w=ssm_w, n_x=len(xs), n_ctx_tiles=B)
    return pl.pallas_call(
        kern,
        grid=(ntok // tm,),
        in_specs=_x_specs(xs, tm, B) + [
            pl.BlockSpec((None, 1, d), lambda i: (layer, 0, 0)),
            pl.BlockSpec((None, None, 1, d), lambda i: (layer, row(i), 0, 0)),
            pl.BlockSpec((None, None, 1, d), lambda i: (layer, row(i), 0, 1)),
            _resident((None, d, n_in), lambda i: (layer, 0, 0)),
            pl.BlockSpec((tm, LANES), lambda i: (tbl(i), 0)),
            pl.BlockSpec((tm, LANES), lambda i: (tbl(i), 0)),
        ],
        out_specs=[
            pl.BlockSpec((tm, attn_w), lambda i: (i, 0)),
            pl.BlockSpec((tm, 4 * LANES), lambda i: (i, 0)),
            pl.BlockSpec((tm, 4 * LANES), lambda i: (i, 0)),
            pl.BlockSpec((None, tm, ssm_w), seq),
            pl.BlockSpec((tm, gm_w), lambda i: (i, 0)),
        ],
        out_shape=[
            jax.ShapeDtypeStruct((ntok, attn_w), BF16),
            jax.ShapeDtypeStruct((ntok, 4 * LANES), BF16),
            jax.ShapeDtypeStruct((ntok, 4 * LANES), BF16),
            jax.ShapeDtypeStruct((B, S, ssm_w), F32),
            jax.ShapeDtypeStruct((ntok, gm_w), F32),
        ],
        compiler_params=_cparams(("arbitrary",)),
        name="in_proj",
    )(*xs, g_mix, mod4, mod4, w_in, cos_t, sin_t)


def _attn_kernel(sink_ref, q_ref, kp_ref, ko_ref, kn_ref, kc_ref, vp_ref, vo_ref, vn_ref, vc_ref,
                 bias_ref, o_ref, s_ref, p_ref, *, layer, n_kv):
    blk = q_ref.shape[0]
    per_kv = GQA_RATIO * HEAD_DIM // LANES
    bias = bias_ref[...]
    nkeys = bias.shape[1]
    low = lax.broadcasted_iota(jnp.int32, (blk, LANES), 1) < HEAD_DIM
    krefs = (kp_ref, ko_ref, kn_ref, kc_ref)
    vrefs = (vp_ref, vo_ref, vn_ref, vc_ref)
    def keys(refs, g, half):
        lo = (2 * g + half) * LANES
        return [r[:, lo:lo + LANES] for r in refs]

    chain_ids = [(g, c0) for g in range(n_kv) for c0 in range(0, per_kv, ATTN_STACK)]
    for ch, (g, c0) in enumerate(chain_ids):
        kb = jnp.concatenate(keys(krefs, g, 0) + keys(krefs, g, 1), axis=0)
        qs = jnp.concatenate([q_ref[:, (g * per_kv + c0 + c) * LANES:(g * per_kv + c0 + c + 1) * LANES]
                              for c in range(ATTN_STACK)], axis=0)
        s_ref[ch] = lax.dot_general(qs, kb, (((1,), (1,)), ((), ())), preferred_element_type=F32)
    for ch, (g, c0) in enumerate(chain_ids):
        vb = jnp.concatenate(keys(vrefs, g, 0) + keys(vrefs, g, 1), axis=0)
        scales = []
        for c in range(ATTN_STACK):
            rows = slice(c * blk, (c + 1) * blk)
            invs = []
            for p in range(2):
                sink = sink_ref[layer, g * GQA_RATIO + 2 * (c0 + c) + p] * LOG2E
                slabs = [slice(p * nkeys + k * LANES, p * nkeys + (k + 1) * LANES) for k in range(nkeys // LANES)]
                mx = None
                for k, sl in enumerate(slabs):
                    sk = s_ref[ch, rows, sl] + bias[:, k * LANES:(k + 1) * LANES]
                    mx = sk if mx is None else jnp.maximum(mx, sk)
                m = jnp.maximum(jnp.max(mx, axis=1, keepdims=True), sink)
                acc = None
                for k, sl in enumerate(slabs):
                    e = jnp.exp2(s_ref[ch, rows, sl] + bias[:, k * LANES:(k + 1) * LANES] - m)
                    p_ref[ch, rows, sl] = e.astype(BF16)
                    acc = e if acc is None else acc + e
                den = jnp.sum(acc, axis=1, keepdims=True) + jnp.exp2(sink - m)
                invs.append(1.0 / den)
            scales.append(jnp.where(low, invs[0], invs[1]))
        o = _dot(p_ref[ch], vb)
        for c in range(ATTN_STACK):
            col = (g * per_kv + c0 + c) * LANES
            o_ref[:, col:col + LANES] = (o[c * blk:(c + 1) * blk] * scales[c]).astype(o_ref.dtype)


def _attn_bias(n_c):
    blk = ATTN_BLOCK
    r = np.arange(blk)[:, None]
    j = np.arange(blk)[None, :]
    zero = np.zeros((blk, blk), np.float32)
    neg = np.full((blk, blk), MASK_NEG, np.float32)
    prev = np.where(j >= r, 0.0, MASK_NEG).astype(np.float32)
    nxt = np.where(j <= r, 0.0, MASK_NEG).astype(np.float32)
    ctx = np.zeros((blk, n_c), np.float32)
    variants = [
        np.concatenate([neg, zero, nxt, ctx], 1),
        np.concatenate([prev, zero, nxt, ctx], 1),
        np.concatenate([prev, zero, neg, ctx], 1),
        np.concatenate([neg, neg, neg, ctx], 1),
    ]
    return jnp.asarray(np.stack(variants))


def _attention(layer, q, kx, vx, sink, bias, dims, skip_ctx):
    B, n_c, L, ntok = dims["B"], dims["n_c"], dims["L"], dims["ntok"]
    blk = ATTN_BLOCK
    attn_w = q.shape[1]
    n_kv = dims["kv_w"] // HEAD_DIM
    nblk = ntok // blk
    nctb = B * n_c // blk
    cpb = n_c // blk
    nb = L // blk
    assert nb >= 2
    off = nctb if skip_ctx else 0

    def bat(t):
        return jnp.where(t < nctb, t // cpb, (t - nctb) // nb)

    def variant(t):
        n = (t - nctb) % nb
        return jnp.where(t < nctb, 3, jnp.where(n == 0, 0, jnp.where(n == nb - 1, 2, 1)))

    kvw = kx.shape[1]
    own = pl.BlockSpec((blk, kvw), lambda t: (t + off, 0))
    prev = pl.BlockSpec((blk, kvw), lambda t: (jnp.maximum(t + off - 1, 0), 0))
    nxt = pl.BlockSpec((blk, kvw), lambda t: (jnp.minimum(t + off + 1, nblk - 1), 0))
    ctx = pl.BlockSpec((n_c, kvw), lambda t: (bat(t + off), 0))
    kern = functools.partial(_attn_kernel, layer=layer, n_kv=n_kv)
    return pl.pallas_call(
        kern,
        grid=(nblk - off,),
        in_specs=[
            pl.BlockSpec(memory_space=pltpu.SMEM),
            pl.BlockSpec((blk, attn_w), lambda t: (t + off, 0)),
            prev, own, nxt, ctx, prev, own, nxt, ctx,
            pl.BlockSpec((None,) + bias.shape[1:], lambda t: (variant(t + off), 0, 0)),
        ],
        out_specs=pl.BlockSpec((blk, attn_w), lambda t: (t, 0)),
        out_shape=jax.ShapeDtypeStruct(((nblk - off) * blk, attn_w), BF16),
        scratch_shapes=[
            pltpu.VMEM((attn_w // LANES // ATTN_STACK, ATTN_STACK * blk, 2 * bias.shape[2]), F32),
            pltpu.VMEM((attn_w // LANES // ATTN_STACK, ATTN_STACK * blk, 2 * bias.shape[2]), BF16),
        ],
        compiler_params=_cparams(("arbitrary",)),
        name="window_attn",
    )(sink, q, kx, kx, kx, kx, vx, vx, vx, vx, bias)


def _s5_prep_kernel(lre_ref, lim_ref, ldt_ref, bre_ref, bim_ref, are_ref, aim_ref, bbre_ref, bbim_ref):
    lre = lre_ref[...]
    lim = lim_ref[...]
    dt = jnp.exp(ldt_ref[...])
    mag = jnp.exp(lre * dt)
    a_re = mag * jnp.cos(lim * dt)
    a_im = mag * jnp.sin(lim * dt)
    den = lre * lre + lim * lim
    n_re = a_re - 1.0
    f_re = ((n_re * lre + a_im * lim) / den)[:, None, :]
    f_im = ((a_im * lre - n_re * lim) / den)[:, None, :]
    are_ref[...] = a_re
    aim_ref[...] = a_im
    bre = bre_ref[...]
    bim = bim_ref[...]
    bbre_ref[...] = f_re * bre - f_im * bim
    bbim_ref[...] = f_re * bim + f_im * bre


def _s5_prepare(lam_re, lam_im, log_dt, b_re, b_im, c_re, c_im, B):
    depth, _, G, P = lam_re.shape
    H = b_re.shape[-1]
    n = depth * 2 * G
    bT_re = jnp.swapaxes(b_re, -1, -2).reshape(n, H, P)
    bT_im = jnp.swapaxes(b_im, -1, -2).reshape(n, H, P)
    a_re, a_im, bb_re, bb_im = pl.pallas_call(
        _s5_prep_kernel,
        out_shape=[jax.ShapeDtypeStruct((n, P), F32), jax.ShapeDtypeStruct((n, P), F32),
                   jax.ShapeDtypeStruct((n, H, P), F32), jax.ShapeDtypeStruct((n, H, P), F32)],
        name="s5_discretize",
    )(lam_re.reshape(n, P), lam_im.reshape(n, P), log_dt.reshape(n, 1), bT_re, bT_im)
    gh = G // 2
    gq = S5_PIECE // P
    nq = gh // gq
    bb = jnp.stack([bb_re, bb_im], axis=2).astype(BF16).reshape(depth, 2, 2, gh, H, 2, P)
    bb = jnp.transpose(bb, (0, 1, 2, 5, 3, 4, 6))
    bb = jnp.concatenate([bb] * gq, axis=-1)[:, :, :, :, None]
    g_row = jnp.arange(gh).reshape(1, gh, 1, 1)
    g_col = gq * jnp.arange(nq).reshape(nq, 1, 1, 1) + (jnp.arange(gq * P) // P).reshape(1, 1, 1, gq * P)
    bm = jnp.where(g_row == g_col, jnp.broadcast_to(bb, (depth, 2, 2, 2, nq, gh, H, gq * P)), 0)
    bm = bm.reshape(depth, 2, 2, 2 * nq, gh * H, gq * P)
    cc = jnp.stack([c_re, -c_im], axis=4).astype(BF16).reshape(depth, 2, 2, nq, gq, H, 2, P)
    cc = jnp.transpose(cc, (0, 1, 2, 6, 3, 4, 7, 5))
    cc = jnp.concatenate([cc] * gh, axis=-1)
    g_own = gq * jnp.arange(nq).reshape(nq, 1, 1, 1) + jnp.arange(gq).reshape(1, gq, 1, 1)
    g_out = (jnp.arange(gh * H) // H).reshape(1, 1, 1, gh * H)
    cm = jnp.where(g_own == g_out, cc, 0)
    cm = cm.reshape(depth, 2, 2, 2 * nq, gq * P, gh * H)
    a_re = jnp.tile(a_re.reshape(depth, 2, 2, gh * P), (1, 1, B, 1))
    a_im = jnp.tile(a_im.reshape(depth, 2, 2, gh * P), (1, 1, B, 1))
    return bm, cm, a_re, a_im


def _s5_kernel(uf0_ref, ufn_ref, ub_ref, bm_ref, cm_ref, are_ref, aim_ref, yf_ref, yb_ref,
               buf_f, buf_b, h_ref, yacc_ref, lhs_ref, *, B, n_chunks):
    T, pitch, shift = S5_T, S5_PITCH, S5_SHIFT
    R = 2 * B
    wide = T + SUBLANES
    hw = bm_ref.shape[3]
    npiece = bm_ref.shape[2]
    sw = are_ref.shape[2]
    nsl = sw // LANES
    per = S5_PIECE // LANES
    steps = T // (2 * npiece)
    s = pl.program_id(0)

    def window(b, k):
        r = 2 * b + k
        return (r * pitch - shift, wide) if k else (r * pitch, T)

    def put_lhs(u_ref):
        for k in range(2):
            parts = []
            for b in range(B):
                ub = u_ref[b, :, k * hw:(k + 1) * hw]
                if k:
                    ub = pltpu.roll(jnp.concatenate([ub, jnp.zeros((SUBLANES, hw), F32)], axis=0), shift, axis=0)
                parts.append(ub)
            rows = B * (wide if k else T)
            lhs_ref[k, 0:rows, :] = jnp.concatenate(parts, axis=0).astype(BF16)

    def bu_piece(buf, d, k, q):
        rows = wide if k else T
        bu = _dot(lhs_ref[k, 0:B * rows, :], bm_ref[d, k, q])
        for b in range(B):
            base, _ = window(b, k)
            for e in range(per):
                buf[per * q + e, base:base + rows, :] = bu[b * rows:(b + 1) * rows, e * LANES:(e + 1) * LANES]

    def readout_piece(buf, d, k, q):
        rows = wide if k else T
        parts = []
        for b in range(B):
            base, _ = window(b, k)
            parts.append(jnp.concatenate([buf[per * q + e, base:base + rows, :] for e in range(per)], axis=1))
        yacc_ref[k, 0:B * rows, :] += _dot(jnp.concatenate(parts, axis=0).astype(BF16), cm_ref[d, k, q])

    def scan_step(buf, d, t, h_re, h_im):
        idx = pl.ds(t, R, stride=pitch)
        b_re = jnp.concatenate([buf[sl, idx, :] for sl in range(nsl)], axis=1)
        b_im = jnp.concatenate([buf[nsl + sl, idx, :] for sl in range(nsl)], axis=1)
        a_re = are_ref[d]
        a_im = aim_ref[d]
        n_re = a_re * h_re - a_im * h_im + b_re
        n_im = a_re * h_im + a_im * h_re + b_im
        for sl in range(nsl):
            buf[sl, idx, :] = n_re[:, sl * LANES:(sl + 1) * LANES]
            buf[nsl + sl, idx, :] = n_im[:, sl * LANES:(sl + 1) * LANES]
        return n_re, n_im

    def slot(d_scan, buf_scan, d_mx, buf_mx, u_mx_ref, y_mx_ref):
        put_lhs(u_mx_ref)
        yacc_ref[...] = jnp.zeros_like(yacc_ref)
        carry = (h_ref[d_scan, 0], h_ref[d_scan, 1])
        for k in range(2):
            h_re, h_im = carry
            for q in range(npiece):
                readout_piece(buf_mx, d_mx, k, q)
                bu_piece(buf_mx, d_mx, k, q)
                for i in range(steps):
                    t = (k * npiece + q) * steps + i
                    h_re, h_im = scan_step(buf_scan, d_scan, T - 1 - t if d_scan else t, h_re, h_im)
            carry = (h_re, h_im)
            rows = wide if k else T
            for b in range(B):
                yb = yacc_ref[k, b * rows:(b + 1) * rows, :]
                if k:
                    yb = pltpu.roll(yb, rows - shift, axis=0)[:T]
                y_mx_ref[b, :, k * hw:(k + 1) * hw] = yb
        h_ref[d_scan, 0] = carry[0]
        h_ref[d_scan, 1] = carry[1]

    @pl.when(s == 0)
    def _():
        h_ref[...] = jnp.zeros_like(h_ref)
        buf_b[...] = jnp.zeros_like(buf_b)
        put_lhs(uf0_ref)
        for k in range(2):
            for q in range(npiece):
                bu_piece(buf_f, 0, k, q)

    slot(0, buf_f, 1, buf_b, ub_ref, yb_ref)

    @pl.when(s < n_chunks)
    def _():
        slot(1, buf_b, 0, buf_f, ufn_ref, yf_ref)


def _s5_scan(layer, su, bm, cm, a_re, a_im, dims):
    B, n_c, L = dims["B"], dims["n_c"], dims["L"]
    T = S5_T
    S = n_c + L
    ssm_w = su.shape[2]
    ncb, nlb = n_c // T, L // T
    R = 2 * B
    sw = a_re.shape[3]
    nslab = 2 * sw // LANES

    def bwd(s):
        return jnp.where(s < ncb, ncb - 1 - s, ncb + (nlb - 1 - (s - ncb)))

    blk = (B, T, ssm_w)
    nch = S // T
    last = nch - 1
    wide_rows = B * (T + SUBLANES)
    kern = functools.partial(_s5_kernel, B=B, n_chunks=nch)
    return pl.pallas_call(
        kern,
        grid=(nch + 1,),
        in_specs=[
            pl.BlockSpec(blk, lambda s: (0, 0, 0)),
            pl.BlockSpec(blk, lambda s: (0, jnp.minimum(s + 1, last), 0)),
            pl.BlockSpec(blk, lambda s: (0, bwd(jnp.minimum(s, last)), 0)),
            _resident((None,) + bm.shape[1:], lambda s: (layer, 0, 0, 0, 0, 0)),
            _resident((None,) + cm.shape[1:], lambda s: (layer, 0, 0, 0, 0, 0)),
            _resident((None,) + a_re.shape[1:], lambda s: (layer, 0, 0, 0)),
            _resident((None,) + a_im.shape[1:], lambda s: (layer, 0, 0, 0)),
        ],
        out_specs=[pl.BlockSpec(blk, lambda s: (0, jnp.minimum(s, last), 0)),
                   pl.BlockSpec(blk, lambda s: (0, bwd(jnp.maximum(s - 1, 0)), 0))],
        out_shape=[jax.ShapeDtypeStruct(su.shape, F32), jax.ShapeDtypeStruct(su.shape, F32)],
        scratch_shapes=[
            pltpu.VMEM((nslab, R * S5_PITCH, LANES), F32),
            pltpu.VMEM((nslab, R * S5_PITCH, LANES), F32),
            pltpu.VMEM((2, 2, R, sw), F32),
            pltpu.VMEM((2, wide_rows, S5_PIECE), F32),
            pltpu.VMEM((2, wide_rows, ssm_w // 2), BF16),
        ],
        compiler_params=_cparams(("arbitrary",)),
        name="s5_scan",
    )(su, su, su, bm, cm, a_re, a_im)


def _mix_out_kernel(*refs, n_x, n_ctx_tiles, off, chains):
    x_refs = refs[:n_x]
    oa_ref = refs[n_x]
    s5_refs = refs[n_x + 1:n_x + 1 + 3 * chains]
    (gm_ref, d_ref, wg_ref, bg_ref, lg_ref, lb_ref, ws_ref, bs_ref, wo_ref, gate_ref, gf_ref, shf_ref, scf_ref,
     o_ref, h_ref) = refs[n_x + 1 + 3 * chains:]
    sub = o_ref.shape[0] // chains
    gw = gm_ref.shape[1] // 2
    bs = bs_ref[...]
    for ch in range(chains):
        rows = slice(ch * sub, (ch + 1) * sub)
        yf_ref, yb_ref, su_ref = s5_refs[3 * ch:3 * ch + 3]
        y = d_ref[...] * su_ref[...] + yf_ref[...] + yb_ref[...]
        g1 = _gelu(y)
        o_ssm = g1 * _sigmoid(_dot(g1.astype(BF16), wg_ref[...]) + bg_ref[...])
        u = _gelu(gm_ref[rows, :gw])
        v = _gelu(gm_ref[rows, gw:])
        mu = jnp.mean(v, axis=-1, keepdims=True)
        var = jnp.mean(jnp.square(v - mu), axis=-1, keepdims=True)
        v = ((v - mu) * lax.rsqrt(var + NORM_EPS) * lg_ref[...] + lb_ref[...]).astype(BF16)
        mixed = []
        for c in range(sub // GMLP_CHUNK):
            cols = []
            for g in range(gw // GMLP_GROUP_W):
                vb = v[c * GMLP_CHUNK:(c + 1) * GMLP_CHUNK, g * GMLP_GROUP_W:(g + 1) * GMLP_GROUP_W]
                cols.append(_dot(ws_ref[g], vb) + bs[:, g:g + 1])
            mixed.append(jnp.concatenate(cols, axis=1))
        o_gmlp = u * jnp.concatenate(mixed, axis=0)
        mix = jnp.concatenate([oa_ref[rows, :], o_ssm.astype(BF16), o_gmlp.astype(BF16)], axis=1)
        x1 = _read_x(x_refs, n_ctx_tiles, off, rows) + gate_ref[...] * _dot(mix, wo_ref[...])
        o_ref[rows, :] = x1
        h_ref[rows, :] = _norm_mod(x1, gf_ref[...], shf_ref[...], scf_ref[...]).astype(BF16)


def _mix_out(layer, xs, o_attn, y_f, y_b, su, gm, p, g_ffn, mod4, dims, skip_ctx):
    B, n_c, L, ntok = dims["B"], dims["n_c"], dims["L"], dims["ntok"]
    d = xs[0].shape[1]
    sub = n_c
    lt = L // sub
    chains = 2 if (B % 2 == 0 and lt % 2 == 0) else 1
    tm = chains * sub
    attn_w, ssm_w, gm_w = o_attn.shape[1], su.shape[2], gm.shape[1]
    ngrp = p["w_s"].shape[1]
    nct = B * n_c // tm
    off = nct if skip_ctx else 0
    nrow = ntok - off * tm
    assert o_attn.shape[0] == nrow

    def row(i):
        return jnp.where(i + off < nct, B, (i + off - nct) // (L // tm))

    def seq(ch):
        def index(i):
            j = (i + off) * chains + ch
            return (jnp.where(j < B, j, (j - B) // lt), jnp.where(j < B, 0, 1 + (j - B) % lt), 0)
        return index

    vec = lambda w: pl.BlockSpec((None, 1, w), lambda i: (layer, 0, 0))
    modspec = lambda k: pl.BlockSpec((None, None, 1, d), lambda i: (layer, row(i), 0, k))
    s5_specs = [pl.BlockSpec((None, sub, ssm_w), seq(ch)) for ch in range(chains) for _ in range(3)]
    s5_args = [a for _ in range(chains) for a in (y_f, y_b, su)]
    kern = functools.partial(_mix_out_kernel, n_x=len(xs), n_ctx_tiles=nct, off=off, chains=chains)
    return pl.pallas_call(
        kern,
        grid=(nrow // tm,),
        in_specs=_x_specs(xs, tm, nct, off) + [pl.BlockSpec((tm, attn_w), lambda i: (i, 0))] + s5_specs + [
            pl.BlockSpec((tm, gm_w), lambda i: (i + off, 0)),
            vec(ssm_w),
            _resident((None, ssm_w, ssm_w), lambda i: (layer, 0, 0)),
            vec(ssm_w), vec(gm_w // 2), vec(gm_w // 2),
            _resident((None, ngrp, GMLP_CHUNK, GMLP_CHUNK), lambda i: (layer, 0, 0, 0)),
            pl.BlockSpec((None, GMLP_CHUNK, ngrp), lambda i: (layer, 0, 0)),
            _resident((None, d, d), lambda i: (layer, 0, 0)),
            modspec(2), vec(d), modspec(3), modspec(4),
        ],
        out_specs=[pl.BlockSpec((tm, d), lambda i: (i, 0)), pl.BlockSpec((tm, d), lambda i: (i, 0))],
        out_shape=[jax.ShapeDtypeStruct((nrow, d), F32), jax.ShapeDtypeStruct((nrow, d), BF16)],
        compiler_params=_cparams(("arbitrary",)),
        name="mix_out",
    )(*xs, o_attn, *s5_args, gm, p["ssm_d"], p["w_glu"], p["b_glu"], p["ln_g"], p["ln_b"],
      p["w_s"], p["b_s_t"], p["w_out"], mod4, g_ffn, mod4, mod4)


def _ffn_kernel(hm_ref, hp_ref, hn_ref, x_ref, gate_ref, gfin_ref, wg_ref, wv_ref, cw_ref, cb_ref, wd_ref,
                o_ref, h_ref, mp_ref, mn_ref, *, n_ctx_tiles, n_c, L, final):
    i = pl.program_id(0)
    j = pl.program_id(1)
    tm = x_ref.shape[0]
    halo = hp_ref.shape[0]

    @pl.when(j == 0)
    def _():
        h_ref[0:halo, :] = hp_ref[...]
        h_ref[halo:halo + tm, :] = hm_ref[...]
        h_ref[halo + tm:, :] = hn_ref[...]
        is_ctx = i < n_ctx_tiles
        seq_len = jnp.where(is_ctx, n_c, L)
        start = lax.rem(jnp.where(is_ctx, i * tm, (i - n_ctx_tiles) * tm), seq_len)
        pos = start + lax.broadcasted_iota(jnp.int32, (tm, LANES), 0)
        first = pos == 0
        last = pos == seq_len - 1
        for k in range(1, tm // min(n_c, L) + 1):
            first = jnp.logical_or(first, pos == k * seq_len)
            last = jnp.logical_or(last, pos == (k + 1) * seq_len - 1)
        mp_ref[...] = jnp.where(first, 0.0, 1.0)
        mn_ref[...] = jnp.where(last, 0.0, 1.0)
        o_ref[...] = jnp.zeros_like(o_ref)

    tf = wg_ref.shape[1]
    rep = tf // LANES
    ge = _dot(h_ref[...], wg_ref[...])
    val = _dot(h_ref[halo:halo + tm, :], wv_ref[...])
    ext = tm + 2 * halo
    g_prev = pltpu.roll(ge, 1, axis=0)[halo:halo + tm]
    g_next = pltpu.roll(ge, ext - 1, axis=0)[halo:halo + tm]
    mp = jnp.concatenate([mp_ref[...]] * rep, axis=1)
    mn = jnp.concatenate([mn_ref[...]] * rep, axis=1)
    cw = cw_ref[...]
    gc = cw[0:1] * (g_prev * mp) + cw[1:2] * ge[halo:halo + tm] + cw[2:3] * (g_next * mn) + cb_ref[...]
    act = (gc * _sigmoid(gc) * val).astype(BF16)
    o_ref[...] += _dot(act, wd_ref[...])

    @pl.when(j == pl.num_programs(1) - 1)
    def _():
        xn = x_ref[...] + gate_ref[...] * o_ref[...]
        if final:
            xn = xn * lax.rsqrt(jnp.mean(xn * xn, axis=-1, keepdims=True) + NORM_EPS) * gfin_ref[...]
        o_ref[...] = xn


def _conv_ffn(layer, x1, h2, mod4, w_up, conv_w, conv_b, w_down, g_final, dims, skip_ctx, final):
    B, n_c, L = dims["B"], dims["n_c"], dims["L"]
    nrow, d = x1.shape
    d_ff = w_down.shape[1]
    tm, tf = dims["ffn_tm"], FFN_TF
    halo = BF16_ROWS
    nf = d_ff // tf
    nct = 0 if skip_ctx else B * n_c // tm
    lt = L // tm
    nhalo = nrow // halo

    def row(i):
        return jnp.where(i < nct, B, (i - nct) // lt)

    modspec = lambda k: pl.BlockSpec((None, None, 1, d), lambda i, j: (layer, row(i), 0, k))
    kern = functools.partial(_ffn_kernel, n_ctx_tiles=nct, n_c=n_c, L=L, final=final)
    return pl.pallas_call(
        kern,
        grid=(nrow // tm, nf),
        in_specs=[
            pl.BlockSpec((tm, d), lambda i, j: (i, 0)),
            pl.BlockSpec((halo, d), lambda i, j: (jnp.maximum(i * (tm // halo) - 1, 0), 0)),
            pl.BlockSpec((halo, d), lambda i, j: (jnp.minimum((i + 1) * (tm // halo), nhalo - 1), 0)),
            pl.BlockSpec((tm, d), lambda i, j: (i, 0)),
            modspec(5),
            pl.BlockSpec((1, d), lambda i, j: (0, 0)),
            pl.BlockSpec((None, d, tf), lambda i, j: (layer, 0, j)),
            pl.BlockSpec((None, d, tf), lambda i, j: (layer, 0, nf + j)),
            pl.BlockSpec((None, 3, tf), lambda i, j: (layer, 0, j)),
            pl.BlockSpec((None, 1, tf), lambda i, j: (layer, 0, j)),
            pl.BlockSpec((None, tf, d), lambda i, j: (layer, j, 0)),
        ],
        out_specs=pl.BlockSpec((tm, d), lambda i, j: (i, 0)),
        out_shape=jax.ShapeDtypeStruct(x1.shape, F32),
        scratch_shapes=[
            pltpu.VMEM((tm + 2 * halo, d), BF16),
            pltpu.VMEM((tm, LANES), F32),
            pltpu.VMEM((tm, LANES), F32),
        ],
        compiler_params=_cparams(("arbitrary", "arbitrary")),
        name="conv_ffn",
    )(h2, h2, h2, x1, mod4, g_final, w_up, w_up, conv_w, conv_b, w_down)


def _rope_tables(n_c, L):
    n_freq = HEAD_DIM // 4
    t = np.arange(L)
    inv_freq = jnp.asarray(ROPE_BASE, F32) ** (-jnp.arange(n_freq, dtype=F32) / n_freq)
    pos = jnp.asarray(np.stack([t // GRID_W, t % GRID_W], axis=-1), F32)
    ang = pos[:, :, None] * inv_freq
    cos, sin = jnp.cos(ang), jnp.sin(ang)
    c64 = jnp.concatenate([cos, cos], axis=-1).reshape(L, HEAD_DIM)
    s64 = jnp.concatenate([-sin, sin], axis=-1).reshape(L, HEAD_DIM)
    reps = LANES // HEAD_DIM
    c = jnp.concatenate([jnp.ones((n_c, LANES), F32), jnp.tile(c64, (1, reps))], axis=0)
    s = jnp.concatenate([jnp.zeros((n_c, LANES), F32), jnp.tile(s64, (1, reps))], axis=0)
    return c, s


def kernel(x, c, ctx, c_ctx, w_ada, b_ada, g_mix, g_ffn, w_in, w_out, attn_sink, ssm_lambda_re, ssm_lambda_im,
           ssm_log_dt, ssm_b_re, ssm_b_im, ssm_c_re, ssm_c_im, ssm_d, ssm_w_glu, ssm_b_glu, gmlp_ln_g, gmlp_ln_b,
           gmlp_w_s, gmlp_b_s, ffn_w_up, ffn_conv_w, ffn_conv_b, ffn_w_down, g_final):
    B, L, d = x.shape
    n_c = ctx.shape[1]
    depth = w_in.shape[0]
    ssm_w = ssm_d.shape[1]
    gmlp_w = gmlp_ln_g.shape[1]
    attn_w = w_in.shape[2] - ssm_w - 2 * gmlp_w
    attn_w = attn_w * GQA_RATIO // (GQA_RATIO + 2)
    kv_w = attn_w // GQA_RATIO
    ntok = B * (n_c + L)
    ffn_tm = min(FFN_TM, B * n_c)
    assert n_c % ATTN_BLOCK == 0 and L % n_c == 0 and L % ffn_tm == 0 and (B * n_c) % ffn_tm == 0
    assert n_c % S5_T == 0 and L % S5_T == 0 and B + 1 <= SUBLANES
    dims = dict(B=B, n_c=n_c, L=L, ntok=ntok, attn_w=attn_w, kv_w=kv_w, ssm_w=ssm_w, ffn_tm=ffn_tm)

    cpad = jnp.zeros((SUBLANES, d), F32).at[:B].set(c).at[B].set(c_ctx)
    mod = _modulation(cpad, w_ada, b_ada)
    mod4 = mod.reshape(depth, SUBLANES, 1, N_MOD * d)

    bm, cm, a_re, a_im = _s5_prepare(ssm_lambda_re, ssm_lambda_im, ssm_log_dt, ssm_b_re, ssm_b_im,
                                     ssm_c_re, ssm_c_im, B)
    cos_t, sin_t = _rope_tables(n_c, L)
    bias = _attn_bias(n_c)

    w_in_b = w_in.astype(BF16)
    w_up_b = ffn_w_up.astype(BF16)
    w_down_b = ffn_w_down.astype(BF16)
    mixp = dict(
        ssm_d=ssm_d.reshape(depth, 1, ssm_w), w_glu=ssm_w_glu.astype(BF16), b_glu=ssm_b_glu.reshape(depth, 1, ssm_w),
        ln_g=gmlp_ln_g.reshape(depth, 1, gmlp_w), ln_b=gmlp_ln_b.reshape(depth, 1, gmlp_w),
        w_s=gmlp_w_s.astype(BF16), b_s_t=jnp.swapaxes(gmlp_b_s, 1, 2), w_out=w_out.astype(BF16))
    g_mix3 = g_mix.reshape(depth, 1, d)
    g_ffn3 = g_ffn.reshape(depth, 1, d)
    conv_b3 = ffn_conv_b.reshape(depth, 1, -1)

    xs = (ctx.reshape(B * n_c, d), x.reshape(B * L, d))
    g_fin = g_final.reshape(1, d)
    for layer in range(depth):
        last = layer == depth - 1
        q, kx, vx, su, gm = _in_proj(layer, xs, g_mix3, mod4, w_in_b, cos_t, sin_t, dims)
        o_attn = _attention(layer, q, kx, vx, attn_sink, bias, dims, skip_ctx=last)
        y_f, y_b = _s5_scan(layer, su, bm, cm, a_re, a_im, dims)
        x1, h2 = _mix_out(layer, xs, o_attn, y_f, y_b, su, gm, mixp, g_ffn3, mod4, dims, skip_ctx=last)
        xs = (_conv_ffn(layer, x1, h2, mod4, w_up_b, ffn_conv_w, conv_b3, w_down_b, g_fin, dims,
                        skip_ctx=last, final=last),)
    return xs[0].reshape(B, L, d)
```

```python
import functools
import math

import numpy as np
import jax
import jax.numpy as jnp
from jax import lax
from jax.experimental import pallas as pl
from jax.experimental.pallas import tpu as pltpu

F32 = jnp.float32
BF16 = jnp.bfloat16

HEAD_DIM = 64
GQA_RATIO = 8
GRID_W = 64
ROPE_BASE = 10000.0
ATTN_BLOCK = 128
SSM_GROUP = 16
SSM_STATE = 64
GMLP_CHUNK = 128
GMLP_GROUP_W = 128
N_MOD = 6
NORM_EPS = 1e-6
MASK_NEG = -1e30
LOG2E = 1.4426950408889634

LANES = 128
SUBLANES = 8
BF16_ROWS = 16

MOD_TN = 1024
FFN_TM = 512
FFN_TF = 1024
ATTN_STACK = 2
S5_T = 128
S5_PITCH = S5_T + 4
S5_SHIFT = 4
S5_PIECE = 256
VMEM_LIMIT = 56 * 1024 * 1024


def _cparams(sem):
    return pltpu.CompilerParams(dimension_semantics=sem, vmem_limit_bytes=VMEM_LIMIT)


def _sigmoid(x):
    return 1.0 / (1.0 + jnp.exp(-x))


def _gelu(x):
    k1 = -2.0 * 0.7978845608028654 * LOG2E
    e = jnp.exp2(x * (k1 + (k1 * 0.044715) * (x * x)))
    return x / (1.0 + e)


def _norm_mod(x, g, shift, scale):
    y = x * lax.rsqrt(jnp.mean(x * x, axis=-1, keepdims=True) + NORM_EPS)
    return y * (g * (1.0 + scale)) + shift


def _dot(a, b):
    return jnp.dot(a, b, preferred_element_type=F32)


def _resident(block_shape, index_map):
    return pl.BlockSpec(block_shape, index_map, pipeline_mode=pl.Buffered(1))


def _mod_kernel(c_ref, w_ref, b_ref, o_ref):
    c = c_ref[...]
    act = (c * _sigmoid(c)).astype(BF16)
    o_ref[...] = _dot(act, w_ref[...].astype(BF16)) + b_ref[...]


def _modulation(cpad, w_ada, b_ada):
    depth, d, n = w_ada.shape
    rows = cpad.shape[0]
    return pl.pallas_call(
        _mod_kernel,
        grid=(depth, n // MOD_TN),
        in_specs=[
            pl.BlockSpec((rows, d), lambda l, j: (0, 0)),
            pl.BlockSpec((None, d, MOD_TN), lambda l, j: (l, 0, j)),
            pl.BlockSpec((None, 1, MOD_TN), lambda l, j: (l, 0, j)),
        ],
        out_specs=pl.BlockSpec((None, rows, MOD_TN), lambda l, j: (l, 0, j)),
        out_shape=jax.ShapeDtypeStruct((depth, rows, n), F32),
        compiler_params=_cparams(("arbitrary", "arbitrary")),
        name="adaln_mod",
    )(cpad, w_ada, b_ada.reshape(depth, 1, n))


def _rope(x, c, s, lane):
    first = (lane % 32) < 16
    partner = jnp.where(first, pltpu.roll(x, LANES - 16, axis=1), pltpu.roll(x, 16, axis=1))
    return x * c + partner * s


def _x_specs(xs, tm, n_ctx_tiles, off=0):
    d = xs[0].shape[1]
    if len(xs) == 1:
        return [pl.BlockSpec((tm, d), lambda i: (i + off, 0))]
    return [pl.BlockSpec((tm, d), lambda i: (jnp.minimum(i + off, n_ctx_tiles - 1), 0)),
            pl.BlockSpec((tm, d), lambda i: (jnp.maximum(i + off - n_ctx_tiles, 0), 0))]


def _read_x(x_refs, n_ctx_tiles, off=0, rows=slice(None)):
    if len(x_refs) == 1:
        return x_refs[0][rows, :]
    return jnp.where(pl.program_id(0) + off < n_ctx_tiles, x_refs[0][rows, :], x_refs[1][rows, :])


def _in_proj_kernel(*refs, attn_w, kv_w, ssm_w, n_x, n_ctx_tiles):
    x = _read_x(refs[:n_x], n_ctx_tiles)
    g_ref, sh_ref, sc_ref, w_ref, cos_ref, sin_ref, q_ref, kx_ref, vx_ref, su_ref, gm_ref = refs[n_x:]
    h = _norm_mod(x, g_ref[...], sh_ref[...], sc_ref[...]).astype(BF16)
    z = _dot(h, w_ref[...])
    tm = z.shape[0]
    c = cos_ref[...]
    s = sin_ref[...]
    lane = lax.broadcasted_iota(jnp.int32, (tm, LANES), 1)
    qscale = HEAD_DIM ** -0.5 * LOG2E
    for j in range(attn_w // LANES):
        zq = z[:, j * LANES:(j + 1) * LANES]
        q_ref[:, j * LANES:(j + 1) * LANES] = (_rope(zq, c, s, lane) * qscale).astype(BF16)
    low = lane < HEAD_DIM
    off_k, off_v = attn_w, attn_w + kv_w
    for src, dst, rot in ((z[:, off_k:off_k + kv_w], kx_ref, True), (z[:, off_v:off_v + kv_w], vx_ref, False)):
        t = _rope(src, c, s, lane) if rot else src
        h0 = jnp.where(low, t, 0.0)
        h1 = jnp.where(low, 0.0, t)
        parts = (h0, pltpu.roll(h0, HEAD_DIM, axis=1), pltpu.roll(h1, HEAD_DIM, axis=1), h1)
        for p, val in enumerate(parts):
            dst[:, p * LANES:(p + 1) * LANES] = val.astype(BF16)
    off_s = attn_w + 2 * kv_w
    su_ref[...] = z[:, off_s:off_s + ssm_w]
    gm_ref[...] = z[:, off_s + ssm_w:]


def _in_proj(layer, xs, g_mix, mod4, w_in, cos_t, sin_t, dims):
    B, n_c, L, ntok = dims["B"], dims["n_c"], dims["L"], dims["ntok"]
    d = xs[0].shape[1]
    n_in = w_in.shape[2]
    attn_w, kv_w, ssm_w = dims["attn_w"], dims["kv_w"], dims["ssm_w"]
    gm_w = n_in - attn_w - 2 * kv_w - ssm_w
    tm = n_c
    lt = L // tm
    S = n_c + L

    def row(i):
        return jnp.where(i < B, B, (i - B) // lt)

    def tbl(i):
        return jnp.where(i < B, 0, 1 + (i - B) % lt)

    def seq(i):
        return (jnp.where(i < B, i, (i - B) // lt), jnp.where(i < B, 0, 1 + (i - B) % lt), 0)

    kern = functools.partial(_in_proj_kernel, attn_w=attn_w, kv_w=kv_w, ssm_w=ssm_w, n_x=len(xs), n_ctx_tiles=B)
    return pl.pallas_call(
        kern,
        grid=(ntok // tm,),
        in_specs=_x_specs(xs, tm, B) + [
            pl.BlockSpec((None, 1, d), lambda i: (layer, 0, 0)),
            pl.BlockSpec((None, None, 1, d), lambda i: (layer, row(i), 0, 0)),
            pl.BlockSpec((None, None, 1, d), lambda i: (layer, row(i), 0, 1)),
            _resident((None, d, n_in), lambda i: (layer, 0, 0)),
            pl.BlockSpec((tm, LANES), lambda i: (tbl(i), 0)),
            pl.BlockSpec((tm, LANES), lambda i: (tbl(i), 0)),
        ],
        out_specs=[
            pl.BlockSpec((tm, attn_w), lambda i: (i, 0)),
            pl.BlockSpec((tm, 4 * LANES), lambda i: (i, 0)),
            pl.BlockSpec((tm, 4 * LANES), lambda i: (i, 0)),
            pl.BlockSpec((None, tm, ssm_w), seq),
            pl.BlockSpec((tm, gm_w), lambda i: (i, 0)),
        ],
        out_shape=[
            jax.ShapeDtypeStruct((ntok, attn_w), BF16),
            jax.ShapeDtypeStruct((ntok, 4 * LANES), BF16),
            jax.ShapeDtypeStruct((ntok, 4 * LANES), BF16),
            jax.ShapeDtypeStruct((B, S, ssm_w), F32),
            jax.ShapeDtypeStruct((ntok, gm_w), F32),
        ],
        compiler_params=_cparams(("arbitrary",)),
        name="in_proj",
    )(*xs, g_mix, mod4, mod4, w_in, cos_t, sin_t)


def _attn_kernel(sink_ref, q_ref, kp_ref, ko_ref, kn_ref, kc_ref, vp_ref, vo_ref, vn_ref, vc_ref,
                 bias_ref, o_ref, s_ref, p_ref, *, layer, n_kv):
    blk = q_ref.shape[0]
    per_kv = GQA_RATIO * HEAD_DIM // LANES
    bias = bias_ref[...]
    nwin = bias.shape[1] // LANES
    nkeys = bias.shape[1] + kc_ref.shape[0]
    low = lax.broadcasted_iota(jnp.int32, (blk, LANES), 1) < HEAD_DIM
    krefs = (kp_ref, ko_ref, kn_ref, kc_ref)
    vrefs = (vp_ref, vo_ref, vn_ref, vc_ref)
    def keys(refs, g, half):
        lo = (2 * g + half) * LANES
        return [r[:, lo:lo + LANES] for r in refs]

    chain_ids = [(g, c0) for g in range(n_kv) for c0 in range(0, per_kv, ATTN_STACK)]
    for ch, (g, c0) in enumerate(chain_ids):
        kb = jnp.concatenate(keys(krefs, g, 0) + keys(krefs, g, 1), axis=0)
        qs = jnp.concatenate([q_ref[:, (g * per_kv + c0 + c) * LANES:(g * per_kv + c0 + c + 1) * LANES]
                              for c in range(ATTN_STACK)], axis=0)
        s_ref[ch] = lax.dot_general(qs, kb, (((1,), (1,)), ((), ())), preferred_element_type=F32)
    for ch, (g, c0) in enumerate(chain_ids):
        vb = jnp.concatenate(keys(vrefs, g, 0) + keys(vrefs, g, 1), axis=0)
        scales = []
        for c in range(ATTN_STACK):
            rows = slice(c * blk, (c + 1) * blk)
            invs = []
            for p in range(2):
                sink = sink_ref[layer, g * GQA_RATIO + 2 * (c0 + c) + p] * LOG2E
                slabs = [slice(p * nkeys + k * LANES, p * nkeys + (k + 1) * LANES) for k in range(nkeys // LANES)]
                def scores(k, sl):
                    sk = s_ref[ch, rows, sl]
                    return sk + bias[:, k * LANES:(k + 1) * LANES] if k < nwin else sk
                mx = None
                for k, sl in enumerate(slabs):
                    sk = scores(k, sl)
                    mx = sk if mx is None else jnp.maximum(mx, sk)
                m = jnp.maximum(jnp.max(mx, axis=1, keepdims=True), sink)
                acc = None
                for k, sl in enumerate(slabs):
                    e = jnp.exp2(scores(k, sl) - m)
                    p_ref[ch, rows, sl] = e.astype(BF16)
                    acc = e if acc is None else acc + e
                den = jnp.sum(acc, axis=1, keepdims=True) + jnp.exp2(sink - m)
                invs.append(1.0 / den)
            scales.append(jnp.where(low, invs[0], invs[1]))
        o = _dot(p_ref[ch], vb)
        for c in range(ATTN_STACK):
            col = (g * per_kv + c0 + c) * LANES
            o_ref[:, col:col + LANES] = (o[c * blk:(c + 1) * blk] * scales[c]).astype(o_ref.dtype)


def _attn_bias():
    blk = ATTN_BLOCK
    r = np.arange(blk)[:, None]
    j = np.arange(blk)[None, :]
    zero = np.zeros((blk, blk), np.float32)
    neg = np.full((blk, blk), MASK_NEG, np.float32)
    prev = np.where(j >= r, 0.0, MASK_NEG).astype(np.float32)
    nxt = np.where(j <= r, 0.0, MASK_NEG).astype(np.float32)
    variants = [
        np.concatenate([neg, zero, nxt], 1),
        np.concatenate([prev, zero, nxt], 1),
        np.concatenate([prev, zero, neg], 1),
        np.concatenate([neg, neg, neg], 1),
    ]
    return jnp.asarray(np.stack(variants))


def _attention(layer, q, kx, vx, sink, bias, dims, skip_ctx):
    B, n_c, L, ntok = dims["B"], dims["n_c"], dims["L"], dims["ntok"]
    blk = ATTN_BLOCK
    attn_w = q.shape[1]
    n_kv = dims["kv_w"] // HEAD_DIM
    nblk = ntok // blk
    nctb = B * n_c // blk
    cpb = n_c // blk
    nb = L // blk
    assert nb >= 2
    off = nctb if skip_ctx else 0

    def bat(t):
        return jnp.where(t < nctb, t // cpb, (t - nctb) // nb)

    def variant(t):
        n = (t - nctb) % nb
        return jnp.where(t < nctb, 3, jnp.where(n == 0, 0, jnp.where(n == nb - 1, 2, 1)))

    kvw = kx.shape[1]
    own = pl.BlockSpec((blk, kvw), lambda t: (t + off, 0))
    prev = pl.BlockSpec((blk, kvw), lambda t: (jnp.maximum(t + off - 1, 0), 0))
    nxt = pl.BlockSpec((blk, kvw), lambda t: (jnp.minimum(t + off + 1, nblk - 1), 0))
    ctx = pl.BlockSpec((n_c, kvw), lambda t: (bat(t + off), 0))
    kern = functools.partial(_attn_kernel, layer=layer, n_kv=n_kv)
    return pl.pallas_call(
        kern,
        grid=(nblk - off,),
        in_specs=[
            pl.BlockSpec(memory_space=pltpu.SMEM),
            pl.BlockSpec((blk, attn_w), lambda t: (t + off, 0)),
            prev, own, nxt, ctx, prev, own, nxt, ctx,
            pl.BlockSpec((None,) + bias.shape[1:], lambda t: (variant(t + off), 0, 0)),
        ],
        out_specs=pl.BlockSpec((blk, attn_w), lambda t: (t, 0)),
        out_shape=jax.ShapeDtypeStruct(((nblk - off) * blk, attn_w), BF16),
        scratch_shapes=[
            pltpu.VMEM((attn_w // LANES // ATTN_STACK, ATTN_STACK * blk, 2 * (bias.shape[2] + n_c)), F32),
            pltpu.VMEM((attn_w // LANES // ATTN_STACK, ATTN_STACK * blk, 2 * (bias.shape[2] + n_c)), BF16),
        ],
        compiler_params=_cparams(("arbitrary",)),
        name="window_attn",
    )(sink, q, kx, kx, kx, kx, vx, vx, vx, vx, bias)


def _s5_prep_kernel(lre_ref, lim_ref, ldt_ref, bre_ref, bim_ref, are_ref, aim_ref, bbre_ref, bbim_ref):
    lre = lre_ref[...]
    lim = lim_ref[...]
    dt = jnp.exp(ldt_ref[...])
    mag = jnp.exp(lre * dt)
    a_re = mag * jnp.cos(lim * dt)
    a_im = mag * jnp.sin(lim * dt)
    den = lre * lre + lim * lim
    n_re = a_re - 1.0
    f_re = ((n_re * lre + a_im * lim) / den)[:, None, :]
    f_im = ((a_im * lre - n_re * lim) / den)[:, None, :]
    are_ref[...] = a_re
    aim_ref[...] = a_im
    bre = bre_ref[...]
    bim = bim_ref[...]
    bbre_ref[...] = f_re * bre - f_im * bim
    bbim_ref[...] = f_re * bim + f_im * bre


def _s5_prepare(lam_re, lam_im, log_dt, b_re, b_im, c_re, c_im, B):
    depth, _, G, P = lam_re.shape
    H = b_re.shape[-1]
    n = depth * 2 * G
    bT_re = jnp.swapaxes(b_re, -1, -2).reshape(n, H, P)
    bT_im = jnp.swapaxes(b_im, -1, -2).reshape(n, H, P)
    a_re, a_im, bb_re, bb_im = pl.pallas_call(
        _s5_prep_kernel,
        out_shape=[jax.ShapeDtypeStruct((n, P), F32), jax.ShapeDtypeStruct((n, P), F32),
                   jax.ShapeDtypeStruct((n, H, P), F32), jax.ShapeDtypeStruct((n, H, P), F32)],
        name="s5_discretize",
    )(lam_re.reshape(n, P), lam_im.reshape(n, P), log_dt.reshape(n, 1), bT_re, bT_im)
    gh = G // 2
    gq = S5_PIECE // P
    nq = gh // gq
    pw = gq * P
    col = np.arange(pw)
    g_idx = np.arange(gh)[None, :, None, None]
    q_idx = np.arange(nq)[:, None, None, None]
    put_b = (col[None, None, None, :] == P * (g_idx - gq * q_idx) + np.arange(P)[None, None, :, None])
    put_b = jnp.asarray(put_b, BF16)
    bb = jnp.stack([bb_re, bb_im], axis=2).astype(BF16).reshape(depth, 2, 2, gh, H, 2, P)
    bm = jnp.einsum("ldkghcp,qgpn->ldkcqghn", bb, put_b).reshape(depth, 2, 2, 2 * nq, gh * H, pw)
    gg_idx = np.arange(gq)[None, :, None, None]
    put_c = (np.arange(gh * H)[None, None, None, :]
             == H * (gq * np.arange(nq)[:, None, None, None] + gg_idx) + np.arange(H)[None, None, :, None])
    put_c = jnp.asarray(put_c, BF16)
    cc = jnp.stack([c_re, -c_im], axis=4).astype(BF16).reshape(depth, 2, 2, nq, gq, H, 2, P)
    cm = jnp.einsum("ldkqghcp,qghn->ldkcqgpn", cc, put_c).reshape(depth, 2, 2, 2 * nq, pw, gh * H)
    a_re = jnp.tile(a_re.reshape(depth, 2, 2, gh * P), (1, 1, B, 1))
    a_im = jnp.tile(a_im.reshape(depth, 2, 2, gh * P), (1, 1, B, 1))
    return bm, cm, a_re, a_im


def _s5_kernel(uf0_ref, ufn_ref, ub_ref, bm_ref, cm_ref, are_ref, aim_ref, yf_ref, yb_ref,
               buf_f, buf_b, h_ref, yacc_ref, lhs_ref, *, B, n_chunks):
    T, pitch, shift = S5_T, S5_PITCH, S5_SHIFT
    R = 2 * B
    wide = T + SUBLANES
    hw = bm_ref.shape[3]
    npiece = bm_ref.shape[2]
    sw = are_ref.shape[2]
    nsl = sw // LANES
    per = S5_PIECE // LANES
    steps = T // (2 * npiece)
    s = pl.program_id(0)

    def window(b, k):
        r = 2 * b + k
        return (r * pitch - shift, wide) if k else (r * pitch, T)

    def put_lhs(u_ref):
        for k in range(2):
            parts = []
            for b in range(B):
                ub = u_ref[b, :, k * hw:(k + 1) * hw]
                if k:
                    ub = pltpu.roll(jnp.concatenate([ub, jnp.zeros((SUBLANES, hw), F32)], axis=0), shift, axis=0)
                parts.append(ub)
            rows = B * (wide if k else T)
            lhs_ref[k, 0:rows, :] = jnp.concatenate(parts, axis=0).astype(BF16)

    def bu_piece(buf, d, k, q):
        rows = wide if k else T
        bu = _dot(lhs_ref[k, 0:B * rows, :], bm_ref[d, k, q])
        for b in range(B):
            base, _ = window(b, k)
            for e in range(per):
                buf[per * q + e, base:base + rows, :] = bu[b * rows:(b + 1) * rows, e * LANES:(e + 1) * LANES]

    def readout_piece(buf, d, k, q):
        rows = wide if k else T
        parts = []
        for b in range(B):
            base, _ = window(b, k)
            parts.append(jnp.concatenate([buf[per * q + e, base:base + rows, :] for e in range(per)], axis=1))
        yacc_ref[k, 0:B * rows, :] += _dot(jnp.concatenate(parts, axis=0).astype(BF16), cm_ref[d, k, q])

    def scan_step(buf, d, t, h_re, h_im):
        idx = pl.ds(t, R, stride=pitch)
        b_re = jnp.concatenate([buf[sl, idx, :] for sl in range(nsl)], axis=1)
        b_im = jnp.concatenate([buf[nsl + sl, idx, :] for sl in range(nsl)], axis=1)
        a_re = are_ref[d]
        a_im = aim_ref[d]
        n_re = a_re * h_re - a_im * h_im + b_re
        n_im = a_re * h_im + a_im * h_re + b_im
        for sl in range(nsl):
            buf[sl, idx, :] = n_re[:, sl * LANES:(sl + 1) * LANES]
            buf[nsl + sl, idx, :] = n_im[:, sl * LANES:(sl + 1) * LANES]
        return n_re, n_im

    def slot(d_scan, buf_scan, d_mx, buf_mx, u_mx_ref, y_mx_ref):
        put_lhs(u_mx_ref)
        yacc_ref[...] = jnp.zeros_like(yacc_ref)
        carry = (h_ref[d_scan, 0], h_ref[d_scan, 1])
        for k in range(2):
            h_re, h_im = carry
            for q in range(npiece):
                readout_piece(buf_mx, d_mx, k, q)
                bu_piece(buf_mx, d_mx, k, q)
                for i in range(steps):
                    t = (k * npiece + q) * steps + i
                    h_re, h_im = scan_step(buf_scan, d_scan, T - 1 - t if d_scan else t, h_re, h_im)
            carry = (h_re, h_im)
            rows = wide if k else T
            for b in range(B):
                yb = yacc_ref[k, b * rows:(b + 1) * rows, :]
                if k:
                    yb = pltpu.roll(yb, rows - shift, axis=0)[:T]
                y_mx_ref[b, :, k * hw:(k + 1) * hw] = yb
        h_ref[d_scan, 0] = carry[0]
        h_ref[d_scan, 1] = carry[1]

    @pl.when(s == 0)
    def _():
        h_ref[...] = jnp.zeros_like(h_ref)
        buf_b[...] = jnp.zeros_like(buf_b)
        put_lhs(uf0_ref)
        for k in range(2):
            for q in range(npiece):
                bu_piece(buf_f, 0, k, q)

    slot(0, buf_f, 1, buf_b, ub_ref, yb_ref)

    @pl.when(s < n_chunks)
    def _():
        slot(1, buf_b, 0, buf_f, ufn_ref, yf_ref)


def _s5_scan(layer, su, bm, cm, a_re, a_im, dims):
    B, n_c, L = dims["B"], dims["n_c"], dims["L"]
    T = S5_T
    S = n_c + L
    ssm_w = su.shape[2]
    ncb, nlb = n_c // T, L // T
    R = 2 * B
    sw = a_re.shape[3]
    nslab = 2 * sw // LANES

    def bwd(s):
        return jnp.where(s < ncb, ncb - 1 - s, ncb + (nlb - 1 - (s - ncb)))

    blk = (B, T, ssm_w)
    nch = S // T
    last = nch - 1
    wide_rows = B * (T + SUBLANES)
    kern = functools.partial(_s5_kernel, B=B, n_chunks=nch)
    return pl.pallas_call(
        kern,
        grid=(nch + 1,),
        in_specs=[
            pl.BlockSpec(blk, lambda s: (0, 0, 0)),
            pl.BlockSpec(blk, lambda s: (0, jnp.minimum(s + 1, last), 0)),
            pl.BlockSpec(blk, lambda s: (0, bwd(jnp.minimum(s, last)), 0)),
            _resident((None,) + bm.shape[1:], lambda s: (layer, 0, 0, 0, 0, 0)),
            _resident((None,) + cm.shape[1:], lambda s: (layer, 0, 0, 0, 0, 0)),
            _resident((None,) + a_re.shape[1:], lambda s: (layer, 0, 0, 0)),
            _resident((None,) + a_im.shape[1:], lambda s: (layer, 0, 0, 0)),
        ],
        out_specs=[pl.BlockSpec(blk, lambda s: (0, jnp.minimum(s, last), 0)),
                   pl.BlockSpec(blk, lambda s: (0, bwd(jnp.maximum(s - 1, 0)), 0))],
        out_shape=[jax.ShapeDtypeStruct(su.shape, F32), jax.ShapeDtypeStruct(su.shape, F32)],
        scratch_shapes=[
            pltpu.VMEM((nslab, R * S5_PITCH, LANES), F32),
            pltpu.VMEM((nslab, R * S5_PITCH, LANES), F32),
            pltpu.VMEM((2, 2, R, sw), F32),
            pltpu.VMEM((2, wide_rows, S5_PIECE), F32),
            pltpu.VMEM((2, wide_rows, ssm_w // 2), BF16),
        ],
        compiler_params=_cparams(("arbitrary",)),
        name="s5_scan",
    )(su, su, su, bm, cm, a_re, a_im)


def _mix_out_kernel(*refs, n_x, n_ctx_tiles, off, chains):
    x_refs = refs[:n_x]
    oa_ref = refs[n_x]
    s5_refs = refs[n_x + 1:n_x + 1 + 3 * chains]
    (gm_ref, d_ref, wg_ref, bg_ref, lg_ref, lb_ref, ws_ref, bs_ref, wo_ref, gate_ref, gf_ref, shf_ref, scf_ref,
     o_ref, h_ref) = refs[n_x + 1 + 3 * chains:]
    sub = o_ref.shape[0] // chains
    gw = gm_ref.shape[1] // 2
    bs = bs_ref[...]
    for ch in range(chains):
        rows = slice(ch * sub, (ch + 1) * sub)
        yf_ref, yb_ref, su_ref = s5_refs[3 * ch:3 * ch + 3]
        y = d_ref[...] * su_ref[...] + yf_ref[...] + yb_ref[...]
        g1 = _gelu(y)
        o_ssm = g1 * _sigmoid(_dot(g1.astype(BF16), wg_ref[...]) + bg_ref[...])
        u = _gelu(gm_ref[rows, :gw])
        v = _gelu(gm_ref[rows, gw:])
        mu = jnp.mean(v, axis=-1, keepdims=True)
        var = jnp.mean(jnp.square(v - mu), axis=-1, keepdims=True)
        v = ((v - mu) * lax.rsqrt(var + NORM_EPS) * lg_ref[...] + lb_ref[...]).astype(BF16)
        mixed = []
        for c in range(sub // GMLP_CHUNK):
            cols = []
            for g in range(gw // GMLP_GROUP_W):
                vb = v[c * GMLP_CHUNK:(c + 1) * GMLP_CHUNK, g * GMLP_GROUP_W:(g + 1) * GMLP_GROUP_W]
                cols.append(_dot(ws_ref[g], vb) + bs[:, g:g + 1])
            mixed.append(jnp.concatenate(cols, axis=1))
        o_gmlp = u * jnp.concatenate(mixed, axis=0)
        mix = jnp.concatenate([oa_ref[rows, :], o_ssm.astype(BF16), o_gmlp.astype(BF16)], axis=1)
        x1 = _read_x(x_refs, n_ctx_tiles, off, rows) + gate_ref[...] * _dot(mix, wo_ref[...])
        o_ref[rows, :] = x1
        h_ref[rows, :] = _norm_mod(x1, gf_ref[...], shf_ref[...], scf_ref[...]).astype(BF16)


def _mix_out(layer, xs, o_attn, y_f, y_b, su, gm, p, g_ffn, mod4, dims, skip_ctx):
    B, n_c, L, ntok = dims["B"], dims["n_c"], dims["L"], dims["ntok"]
    d = xs[0].shape[1]
    sub = n_c
    lt = L // sub
    chains = 2 if (B % 2 == 0 and lt % 2 == 0) else 1
    tm = chains * sub
    attn_w, ssm_w, gm_w = o_attn.shape[1], su.shape[2], gm.shape[1]
    ngrp = p["w_s"].shape[1]
    nct = B * n_c // tm
    off = nct if skip_ctx else 0
    nrow = ntok - off * tm
    assert o_attn.shape[0] == nrow

    def row(i):
        return jnp.where(i + off < nct, B, (i + off - nct) // (L // tm))

    def seq(ch):
        def index(i):
            j = (i + off) * chains + ch
            return (jnp.where(j < B, j, (j - B) // lt), jnp.where(j < B, 0, 1 + (j - B) % lt), 0)
        return index

    vec = lambda w: pl.BlockSpec((None, 1, w), lambda i: (layer, 0, 0))
    modspec = lambda k: pl.BlockSpec((None, None, 1, d), lambda i: (layer, row(i), 0, k))
    s5_specs = [pl.BlockSpec((None, sub, ssm_w), seq(ch)) for ch in range(chains) for _ in range(3)]
    s5_args = [a for _ in range(chains) for a in (y_f, y_b, su)]
    kern = functools.partial(_mix_out_kernel, n_x=len(xs), n_ctx_tiles=nct, off=off, chains=chains)
    return pl.pallas_call(
        kern,
        grid=(nrow // tm,),
        in_specs=_x_specs(xs, tm, nct, off) + [pl.BlockSpec((tm, attn_w), lambda i: (i, 0))] + s5_specs + [
            pl.BlockSpec((tm, gm_w), lambda i: (i + off, 0)),
            vec(ssm_w),
            _resident((None, ssm_w, ssm_w), lambda i: (layer, 0, 0)),
            vec(ssm_w), vec(gm_w // 2), vec(gm_w // 2),
            _resident((None, ngrp, GMLP_CHUNK, GMLP_CHUNK), lambda i: (layer, 0, 0, 0)),
            pl.BlockSpec((None, GMLP_CHUNK, ngrp), lambda i: (layer, 0, 0)),
            _resident((None, d, d), lambda i: (layer, 0, 0)),
            modspec(2), vec(d), modspec(3), modspec(4),
        ],
        out_specs=[pl.BlockSpec((tm, d), lambda i: (i, 0)), pl.BlockSpec((tm, d), lambda i: (i, 0))],
        out_shape=[jax.ShapeDtypeStruct((nrow, d), F32), jax.ShapeDtypeStruct((nrow, d), BF16)],
        compiler_params=_cparams(("arbitrary",)),
        name="mix_out",
    )(*xs, o_attn, *s5_args, gm, p["ssm_d"], p["w_glu"], p["b_glu"], p["ln_g"], p["ln_b"],
      p["w_s"], p["b_s_t"], p["w_out"], mod4, g_ffn, mod4, mod4)


def _ffn_kernel(hm_ref, hp_ref, hn_ref, x_ref, gate_ref, gfin_ref, wg_ref, wv_ref, cw_ref, cb_ref, wd_ref,
                o_ref, h_ref, mp_ref, mn_ref, *, n_ctx_tiles, n_c, L, final, last_width):
    i = pl.program_id(0)
    j = pl.program_id(1)
    tm = x_ref.shape[0]
    halo = hp_ref.shape[0]

    @pl.when(j == 0)
    def _():
        h_ref[0:halo, :] = hp_ref[...]
        h_ref[halo:halo + tm, :] = hm_ref[...]
        h_ref[halo + tm:, :] = hn_ref[...]
        is_ctx = i < n_ctx_tiles
        seq_len = jnp.where(is_ctx, n_c, L)
        start = lax.rem(jnp.where(is_ctx, i * tm, (i - n_ctx_tiles) * tm), seq_len)
        pos = start + lax.broadcasted_iota(jnp.int32, (tm, LANES), 0)
        first = pos == 0
        last = pos == seq_len - 1
        for k in range(1, tm // min(n_c, L) + 1):
            first = jnp.logical_or(first, pos == k * seq_len)
            last = jnp.logical_or(last, pos == (k + 1) * seq_len - 1)
        mp_ref[...] = jnp.where(first, 0.0, 1.0)
        mn_ref[...] = jnp.where(last, 0.0, 1.0)
        o_ref[...] = jnp.zeros_like(o_ref)

    def hidden_tile(width):
        rep = width // LANES
        ge = _dot(h_ref[...], wg_ref[:, 0:width])
        val = _dot(h_ref[halo:halo + tm, :], wv_ref[:, 0:width])
        ext = tm + 2 * halo
        g_prev = pltpu.roll(ge, 1, axis=0)[halo:halo + tm]
        g_next = pltpu.roll(ge, ext - 1, axis=0)[halo:halo + tm]
        mp = jnp.concatenate([mp_ref[...]] * rep, axis=1)
        mn = jnp.concatenate([mn_ref[...]] * rep, axis=1)
        cw = cw_ref[:, 0:width]
        gc = (cw[0:1] * (g_prev * mp) + cw[1:2] * ge[halo:halo + tm] + cw[2:3] * (g_next * mn)
              + cb_ref[:, 0:width])
        act = (gc * _sigmoid(gc) * val).astype(BF16)
        o_ref[...] += _dot(act, wd_ref[0:width, :])

    tf = wg_ref.shape[1]
    if last_width == tf:
        hidden_tile(tf)
    else:
        pl.when(j < pl.num_programs(1) - 1)(lambda: hidden_tile(tf))
        pl.when(j == pl.num_programs(1) - 1)(lambda: hidden_tile(last_width))

    @pl.when(j == pl.num_programs(1) - 1)
    def _():
        xn = x_ref[...] + gate_ref[...] * o_ref[...]
        if final:
            xn = xn * lax.rsqrt(jnp.mean(xn * xn, axis=-1, keepdims=True) + NORM_EPS) * gfin_ref[...]
        o_ref[...] = xn


def _conv_ffn(layer, x1, h2, mod4, w_gate, w_val, conv_w, conv_b, w_down, g_final, dims, skip_ctx, final):
    B, n_c, L = dims["B"], dims["n_c"], dims["L"]
    nrow, d = x1.shape
    d_ff = w_down.shape[1]
    tm, tf = dims["ffn_tm"], FFN_TF
    halo = BF16_ROWS
    nf = pl.cdiv(d_ff, tf)
    last_width = d_ff - (nf - 1) * tf
    assert last_width % LANES == 0
    nct = 0 if skip_ctx else B * n_c // tm
    lt = L // tm
    nhalo = nrow // halo

    def row(i):
        return jnp.where(i < nct, B, (i - nct) // lt)

    modspec = lambda k: pl.BlockSpec((None, None, 1, d), lambda i, j: (layer, row(i), 0, k))
    kern = functools.partial(_ffn_kernel, n_ctx_tiles=nct, n_c=n_c, L=L, final=final, last_width=last_width)
    return pl.pallas_call(
        kern,
        grid=(nrow // tm, nf),
        in_specs=[
            pl.BlockSpec((tm, d), lambda i, j: (i, 0)),
            pl.BlockSpec((halo, d), lambda i, j: (jnp.maximum(i * (tm // halo) - 1, 0), 0)),
            pl.BlockSpec((halo, d), lambda i, j: (jnp.minimum((i + 1) * (tm // halo), nhalo - 1), 0)),
            pl.BlockSpec((tm, d), lambda i, j: (i, 0)),
            modspec(5),
            pl.BlockSpec((1, d), lambda i, j: (0, 0)),
            pl.BlockSpec((None, d, tf), lambda i, j: (layer, 0, j)),
            pl.BlockSpec((None, d, tf), lambda i, j: (layer, 0, j)),
            pl.BlockSpec((None, 3, tf), lambda i, j: (layer, 0, j)),
            pl.BlockSpec((None, 1, tf), lambda i, j: (layer, 0, j)),
            pl.BlockSpec((None, tf, d), lambda i, j: (layer, j, 0)),
        ],
        out_specs=pl.BlockSpec((tm, d), lambda i, j: (i, 0)),
        out_shape=jax.ShapeDtypeStruct(x1.shape, F32),
        scratch_shapes=[
            pltpu.VMEM((tm + 2 * halo, d), BF16),
            pltpu.VMEM((tm, LANES), F32),
            pltpu.VMEM((tm, LANES), F32),
        ],
        compiler_params=_cparams(("arbitrary", "arbitrary")),
        name="conv_ffn",
    )(h2, h2, h2, x1, mod4, g_final, w_gate, w_val, conv_w, conv_b, w_down)


def _rope_tables(n_c, L):
    n_freq = HEAD_DIM // 4
    t = np.arange(L)
    inv_freq = jnp.asarray(ROPE_BASE, F32) ** (-jnp.arange(n_freq, dtype=F32) / n_freq)
    pos = jnp.asarray(np.stack([t // GRID_W, t % GRID_W], axis=-1), F32)
    ang = pos[:, :, None] * inv_freq
    cos, sin = jnp.cos(ang), jnp.sin(ang)
    c64 = jnp.concatenate([cos, cos], axis=-1).reshape(L, HEAD_DIM)
    s64 = jnp.concatenate([-sin, sin], axis=-1).reshape(L, HEAD_DIM)
    reps = LANES // HEAD_DIM
    c = jnp.concatenate([jnp.ones((n_c, LANES), F32), jnp.tile(c64, (1, reps))], axis=0)
    s = jnp.concatenate([jnp.zeros((n_c, LANES), F32), jnp.tile(s64, (1, reps))], axis=0)
    return c, s


def kernel(x, c, ctx, c_ctx, w_ada, b_ada, g_mix, g_ffn, w_in, w_out, attn_sink, ssm_lambda_re, ssm_lambda_im,
           ssm_log_dt, ssm_b_re, ssm_b_im, ssm_c_re, ssm_c_im, ssm_d, ssm_w_glu, ssm_b_glu, gmlp_ln_g, gmlp_ln_b,
           gmlp_w_s, gmlp_b_s, ffn_w_up, ffn_conv_w, ffn_conv_b, ffn_w_down, g_final):
    B, L, d = x.shape
    n_c = ctx.shape[1]
    depth = w_in.shape[0]
    ssm_w = ssm_d.shape[1]
    gmlp_w = gmlp_ln_g.shape[1]
    attn_w = w_in.shape[2] - ssm_w - 2 * gmlp_w
    attn_w = attn_w * GQA_RATIO // (GQA_RATIO + 2)
    kv_w = attn_w // GQA_RATIO
    ntok = B * (n_c + L)
    ffn_tm = min(FFN_TM, B * n_c)
    assert n_c % ATTN_BLOCK == 0 and L % n_c == 0 and L % ffn_tm == 0 and (B * n_c) % ffn_tm == 0
    assert n_c % S5_T == 0 and L % S5_T == 0 and B + 1 <= SUBLANES
    dims = dict(B=B, n_c=n_c, L=L, ntok=ntok, attn_w=attn_w, kv_w=kv_w, ssm_w=ssm_w, ffn_tm=ffn_tm)

    cpad = jnp.zeros((SUBLANES, d), F32).at[:B].set(c).at[B].set(c_ctx)
    mod = _modulation(cpad, w_ada, b_ada)
    mod4 = mod.reshape(depth, SUBLANES, 1, N_MOD * d)

    bm, cm, a_re, a_im = _s5_prepare(ssm_lambda_re, ssm_lambda_im, ssm_log_dt, ssm_b_re, ssm_b_im,
                                     ssm_c_re, ssm_c_im, B)
    cos_t, sin_t = _rope_tables(n_c, L)
    bias = _attn_bias()

    w_in_b = w_in.astype(BF16)
    d_ff = ffn_w_down.shape[1]
    w_gate_b = ffn_w_up[:, :, :d_ff].astype(BF16)
    w_val_b = ffn_w_up[:, :, d_ff:].astype(BF16)
    w_down_b = ffn_w_down.astype(BF16)
    mixp = dict(
        ssm_d=ssm_d.reshape(depth, 1, ssm_w), w_glu=ssm_w_glu.astype(BF16), b_glu=ssm_b_glu.reshape(depth, 1, ssm_w),
        ln_g=gmlp_ln_g.reshape(depth, 1, gmlp_w), ln_b=gmlp_ln_b.reshape(depth, 1, gmlp_w),
        w_s=gmlp_w_s.astype(BF16), b_s_t=jnp.swapaxes(gmlp_b_s, 1, 2), w_out=w_out.astype(BF16))
    g_mix3 = g_mix.reshape(depth, 1, d)
    g_ffn3 = g_ffn.reshape(depth, 1, d)
    conv_b3 = ffn_conv_b.reshape(depth, 1, -1)

    xs = (ctx.reshape(B * n_c, d), x.reshape(B * L, d))
    g_fin = g_final.reshape(1, d)
    for layer in range(depth):
        last = layer == depth - 1
        q, kx, vx, su, gm = _in_proj(layer, xs, g_mix3, mod4, w_in_b, cos_t, sin_t, dims)
        o_attn = _attention(layer, q, kx, vx, attn_sink, bias, dims, skip_ctx=last)
        y_f, y_b = _s5_scan(layer, su, bm, cm, a_re, a_im, dims)
        x1, h2 = _mix_out(layer, xs, o_attn, y_f, y_b, su, gm, mixp, g_ffn3, mod4, dims, skip_ctx=last)
        xs = (_conv_ffn(layer, x1, h2, mod4, w_gate_b, w_val_b, ffn_conv_w, conv_b3, w_down_b, g_fin, dims,
                        skip_ctx=last, final=last),)
    return xs[0].reshape(B, L, d)
```

```python
import functools
import math

import numpy as np
import jax
import jax.numpy as jnp
from jax import lax
from jax.experimental import pallas as pl
from jax.experimental.pallas import tpu as pltpu

F32 = jnp.float32
BF16 = jnp.bfloat16

HEAD_DIM = 64
GQA_RATIO = 8
GRID_W = 64
ROPE_BASE = 10000.0
ATTN_BLOCK = 128
SSM_GROUP = 16
SSM_STATE = 64
GMLP_CHUNK = 128
GMLP_GROUP_W = 128
N_MOD = 6
NORM_EPS = 1e-6
MASK_NEG = -1e30
LOG2E = 1.4426950408889634

LANES = 128
SUBLANES = 8
BF16_ROWS = 16

MOD_TN = 1024
FFN_TM = 512
FFN_TF = 512
ATTN_STACK = 2
S5_T = 128
S5_PITCH = S5_T + 4
S5_SHIFT = 4
S5_PIECE = 256
VMEM_LIMIT = 56 * 1024 * 1024


def _cparams(sem):
    return pltpu.CompilerParams(dimension_semantics=sem, vmem_limit_bytes=VMEM_LIMIT)


def _sigmoid(x):
    return 1.0 / (1.0 + jnp.exp(-x))


def _gelu(x):
    k1 = -2.0 * 0.7978845608028654 * LOG2E
    e = jnp.exp2(x * (k1 + (k1 * 0.044715) * (x * x)))
    return x / (1.0 + e)


def _norm_mod(x, g, shift, scale):
    y = x * lax.rsqrt(jnp.mean(x * x, axis=-1, keepdims=True) + NORM_EPS)
    return y * (g * (1.0 + scale)) + shift


def _dot(a, b):
    return jnp.dot(a, b, preferred_element_type=F32)


def _resident(block_shape, index_map):
    return pl.BlockSpec(block_shape, index_map, pipeline_mode=pl.Buffered(1))


def _mod_kernel(c_ref, w_ref, b_ref, o_ref):
    c = c_ref[...]
    act = (c * _sigmoid(c)).astype(BF16)
    o_ref[...] = _dot(act, w_ref[...].astype(BF16)) + b_ref[...]


def _modulation(cpad, w_ada, b_ada):
    depth, d, n = w_ada.shape
    rows = cpad.shape[0]
    return pl.pallas_call(
        _mod_kernel,
        grid=(depth, n // MOD_TN),
        in_specs=[
            pl.BlockSpec((rows, d), lambda l, j: (0, 0)),
            pl.BlockSpec((None, d, MOD_TN), lambda l, j: (l, 0, j)),
            pl.BlockSpec((None, 1, MOD_TN), lambda l, j: (l, 0, j)),
        ],
        out_specs=pl.BlockSpec((None, rows, MOD_TN), lambda l, j: (l, 0, j)),
        out_shape=jax.ShapeDtypeStruct((depth, rows, n), F32),
        compiler_params=_cparams(("arbitrary", "arbitrary")),
        name="adaln_mod",
    )(cpad, w_ada, b_ada.reshape(depth, 1, n))


def _rope(x, c, s, lane):
    first = (lane % 32) < 16
    partner = jnp.where(first, pltpu.roll(x, LANES - 16, axis=1), pltpu.roll(x, 16, axis=1))
    return x * c + partner * s


def _x_specs(xs, tm, n_ctx_tiles, off=0):
    d = xs[0].shape[1]
    if len(xs) == 1:
        return [pl.BlockSpec((tm, d), lambda i: (i + off, 0))]
    return [pl.BlockSpec((tm, d), lambda i: (jnp.minimum(i + off, n_ctx_tiles - 1), 0)),
            pl.BlockSpec((tm, d), lambda i: (jnp.maximum(i + off - n_ctx_tiles, 0), 0))]


def _read_x(x_refs, n_ctx_tiles, off=0, rows=slice(None)):
    if len(x_refs) == 1:
        return x_refs[0][rows, :]
    return jnp.where(pl.program_id(0) + off < n_ctx_tiles, x_refs[0][rows, :], x_refs[1][rows, :])


def _in_proj_kernel(*refs, attn_w, kv_w, ssm_w, n_x, n_ctx_tiles):
    x = _read_x(refs[:n_x], n_ctx_tiles)
    g_ref, sh_ref, sc_ref, w_ref, cos_ref, sin_ref, q_ref, kx_ref, vx_ref, su_ref, gm_ref = refs[n_x:]
    h = _norm_mod(x, g_ref[...], sh_ref[...], sc_ref[...]).astype(BF16)
    z = _dot(h, w_ref[...])
    tm = z.shape[0]
    c = cos_ref[...]
    s = sin_ref[...]
    lane = lax.broadcasted_iota(jnp.int32, (tm, LANES), 1)
    qscale = HEAD_DIM ** -0.5 * LOG2E
    for j in range(attn_w // LANES):
        zq = z[:, j * LANES:(j + 1) * LANES]
        q_ref[:, j * LANES:(j + 1) * LANES] = (_rope(zq, c, s, lane) * qscale).astype(BF16)
    low = lane < HEAD_DIM
    off_k, off_v = attn_w, attn_w + kv_w
    for src, dst, rot in ((z[:, off_k:off_k + kv_w], kx_ref, True), (z[:, off_v:off_v + kv_w], vx_ref, False)):
        t = _rope(src, c, s, lane) if rot else src
        h0 = jnp.where(low, t, 0.0)
        h1 = jnp.where(low, 0.0, t)
        parts = (h0, pltpu.roll(h0, HEAD_DIM, axis=1), pltpu.roll(h1, HEAD_DIM, axis=1), h1)
        for p, val in enumerate(parts):
            dst[:, p * LANES:(p + 1) * LANES] = val.astype(BF16)
    off_s = attn_w + 2 * kv_w
    su_ref[...] = z[:, off_s:off_s + ssm_w]
    gm_ref[...] = z[:, off_s + ssm_w:]


def _in_proj(layer, xs, g_mix, mod4, w_in, cos_t, sin_t, dims):
    B, n_c, L, ntok = dims["B"], dims["n_c"], dims["L"], dims["ntok"]
    d = xs[0].shape[1]
    n_in = w_in.shape[2]
    attn_w, kv_w, ssm_w = dims["attn_w"], dims["kv_w"], dims["ssm_w"]
    gm_w = n_in - attn_w - 2 * kv_w - ssm_w
    tm = n_c
    lt = L // tm
    S = n_c + L

    def row(i):
        return jnp.where(i < B, B, (i - B) // lt)

    def tbl(i):
        return jnp.where(i < B, 0, 1 + (i - B) % lt)

    def seq(i):
        return (jnp.where(i < B, i, (i - B) // lt), jnp.where(i < B, 0, 1 + (i - B) % lt), 0)

    kern = functools.partial(_in_proj_kernel, attn_w=attn_w, kv_w=kv_w, ssm_w=ssm_w, n_x=len(xs), n_ctx_tiles=B)
    return pl.pallas_call(
        kern,
        grid=(ntok // tm,),
        in_specs=_x_specs(xs, tm, B) + [
            pl.BlockSpec((None, 1, d), lambda i: (layer, 0, 0)),
            pl.BlockSpec((None, None, 1, d), lambda i: (layer, row(i), 0, 0)),
            pl.BlockSpec((None, None, 1, d), lambda i: (layer, row(i), 0, 1)),
            _resident((None, d, n_in), lambda i: (layer, 0, 0)),
            pl.BlockSpec((tm, LANES), lambda i: (tbl(i), 0)),
            pl.BlockSpec((tm, LANES), lambda i: (tbl(i), 0)),
        ],
        out_specs=[
            pl.BlockSpec((tm, attn_w), lambda i: (i, 0)),
            pl.BlockSpec((tm, 4 * LANES), lambda i: (i, 0)),
            pl.BlockSpec((tm, 4 * LANES), lambda i: (i, 0)),
            pl.BlockSpec((None, tm, ssm_w), seq),
            pl.BlockSpec((tm, gm_w), lambda i: (i, 0)),
        ],
        out_shape=[
            jax.ShapeDtypeStruct((ntok, attn_w), BF16),
            jax.ShapeDtypeStruct((ntok, 4 * LANES), BF16),
            jax.ShapeDtypeStruct((ntok, 4 * LANES), BF16),
            jax.ShapeDtypeStruct((B, S, ssm_w), F32),
            jax.ShapeDtypeStruct((ntok, gm_w), F32),
        ],
        compiler_params=_cparams(("arbitrary",)),
        name="in_proj",
    )(*xs, g_mix, mod4, mod4, w_in, cos_t, sin_t)


def _attn_kernel(sink_ref, q_ref, kp_ref, ko_ref, kn_ref, kc_ref, vp_ref, vo_ref, vn_ref, vc_ref,
                 bias_ref, o_ref, s_ref, p_ref, *, layer, n_kv):
    blk = q_ref.shape[0]
    per_kv = GQA_RATIO * HEAD_DIM // LANES
    bias = bias_ref[...]
    nwin = bias.shape[1] // LANES
    nkeys = bias.shape[1] + kc_ref.shape[0]
    low = lax.broadcasted_iota(jnp.int32, (blk, LANES), 1) < HEAD_DIM
    krefs = (kp_ref, ko_ref, kn_ref, kc_ref)
    vrefs = (vp_ref, vo_ref, vn_ref, vc_ref)
    def keys(refs, g, half):
        lo = (2 * g + half) * LANES
        return [r[:, lo:lo + LANES] for r in refs]

    chain_ids = [(g, c0) for g in range(n_kv) for c0 in range(0, per_kv, ATTN_STACK)]
    for ch, (g, c0) in enumerate(chain_ids):
        kb = jnp.concatenate(keys(krefs, g, 0) + keys(krefs, g, 1), axis=0)
        qs = jnp.concatenate([q_ref[:, (g * per_kv + c0 + c) * LANES:(g * per_kv + c0 + c + 1) * LANES]
                              for c in range(ATTN_STACK)], axis=0)
        s_ref[ch] = lax.dot_general(qs, kb, (((1,), (1,)), ((), ())), preferred_element_type=F32)
    for ch, (g, c0) in enumerate(chain_ids):
        vb = jnp.concatenate(keys(vrefs, g, 0) + keys(vrefs, g, 1), axis=0)
        scales = []
        for c in range(ATTN_STACK):
            rows = slice(c * blk, (c + 1) * blk)
            invs = []
            for p in range(2):
                sink = sink_ref[layer, g * GQA_RATIO + 2 * (c0 + c) + p] * LOG2E
                slabs = [slice(p * nkeys + k * LANES, p * nkeys + (k + 1) * LANES) for k in range(nkeys // LANES)]
                def scores(k, sl):
                    sk = s_ref[ch, rows, sl]
                    return sk + bias[:, k * LANES:(k + 1) * LANES] if k < nwin else sk
                mx = None
                for k, sl in enumerate(slabs):
                    sk = scores(k, sl)
                    mx = sk if mx is None else jnp.maximum(mx, sk)
                m = jnp.maximum(jnp.max(mx, axis=1, keepdims=True), sink)
                acc = None
                for k, sl in enumerate(slabs):
                    e = jnp.exp2(scores(k, sl) - m)
                    p_ref[ch, rows, sl] = e.astype(BF16)
                    acc = e if acc is None else acc + e
                den = jnp.sum(acc, axis=1, keepdims=True) + jnp.exp2(sink - m)
                invs.append(1.0 / den)
            scales.append(jnp.where(low, invs[0], invs[1]))
        o = _dot(p_ref[ch], vb)
        for c in range(ATTN_STACK):
            col = (g * per_kv + c0 + c) * LANES
            o_ref[:, col:col + LANES] = (o[c * blk:(c + 1) * blk] * scales[c]).astype(o_ref.dtype)


def _attn_bias():
    blk = ATTN_BLOCK
    r = np.arange(blk)[:, None]
    j = np.arange(blk)[None, :]
    zero = np.zeros((blk, blk), np.float32)
    neg = np.full((blk, blk), MASK_NEG, np.float32)
    prev = np.where(j >= r, 0.0, MASK_NEG).astype(np.float32)
    nxt = np.where(j <= r, 0.0, MASK_NEG).astype(np.float32)
    variants = [
        np.concatenate([neg, zero, nxt], 1),
        np.concatenate([prev, zero, nxt], 1),
        np.concatenate([prev, zero, neg], 1),
        np.concatenate([neg, neg, neg], 1),
    ]
    return jnp.asarray(np.stack(variants))


def _attention(layer, q, kx, vx, sink, bias, dims, skip_ctx):
    B, n_c, L, ntok = dims["B"], dims["n_c"], dims["L"], dims["ntok"]
    blk = ATTN_BLOCK
    attn_w = q.shape[1]
    n_kv = dims["kv_w"] // HEAD_DIM
    nblk = ntok // blk
    nctb = B * n_c // blk
    cpb = n_c // blk
    nb = L // blk
    assert nb >= 2
    off = nctb if skip_ctx else 0

    def bat(t):
        return jnp.where(t < nctb, t // cpb, (t - nctb) // nb)

    def variant(t):
        n = (t - nctb) % nb
        return jnp.where(t < nctb, 3, jnp.where(n == 0, 0, jnp.where(n == nb - 1, 2, 1)))

    kvw = kx.shape[1]
    own = pl.BlockSpec((blk, kvw), lambda t: (t + off, 0))
    prev = pl.BlockSpec((blk, kvw), lambda t: (jnp.maximum(t + off - 1, 0), 0))
    nxt = pl.BlockSpec((blk, kvw), lambda t: (jnp.minimum(t + off + 1, nblk - 1), 0))
    ctx = pl.BlockSpec((n_c, kvw), lambda t: (bat(t + off), 0))
    kern = functools.partial(_attn_kernel, layer=layer, n_kv=n_kv)
    return pl.pallas_call(
        kern,
        grid=(nblk - off,),
        in_specs=[
            pl.BlockSpec(memory_space=pltpu.SMEM),
            pl.BlockSpec((blk, attn_w), lambda t: (t + off, 0)),
            prev, own, nxt, ctx, prev, own, nxt, ctx,
            pl.BlockSpec((None,) + bias.shape[1:], lambda t: (variant(t + off), 0, 0)),
        ],
        out_specs=pl.BlockSpec((blk, attn_w), lambda t: (t, 0)),
        out_shape=jax.ShapeDtypeStruct(((nblk - off) * blk, attn_w), BF16),
        scratch_shapes=[
            pltpu.VMEM((attn_w // LANES // ATTN_STACK, ATTN_STACK * blk, 2 * (bias.shape[2] + n_c)), F32),
            pltpu.VMEM((attn_w // LANES // ATTN_STACK, ATTN_STACK * blk, 2 * (bias.shape[2] + n_c)), BF16),
        ],
        compiler_params=_cparams(("arbitrary",)),
        name="window_attn",
    )(sink, q, kx, kx, kx, kx, vx, vx, vx, vx, bias)


def _s5_prep_kernel(lre_ref, lim_ref, ldt_ref, bre_ref, bim_ref, are_ref, aim_ref, bbre_ref, bbim_ref):
    lre = lre_ref[...]
    lim = lim_ref[...]
    dt = jnp.exp(ldt_ref[...])
    mag = jnp.exp(lre * dt)
    a_re = mag * jnp.cos(lim * dt)
    a_im = mag * jnp.sin(lim * dt)
    den = lre * lre + lim * lim
    n_re = a_re - 1.0
    f_re = ((n_re * lre + a_im * lim) / den)[:, None, :]
    f_im = ((a_im * lre - n_re * lim) / den)[:, None, :]
    are_ref[...] = a_re
    aim_ref[...] = a_im
    bre = bre_ref[...]
    bim = bim_ref[...]
    bbre_ref[...] = f_re * bre - f_im * bim
    bbim_ref[...] = f_re * bim + f_im * bre


def _s5_prepare(lam_re, lam_im, log_dt, b_re, b_im, c_re, c_im, B):
    depth, _, G, P = lam_re.shape
    H = b_re.shape[-1]
    n = depth * 2 * G
    bT_re = jnp.swapaxes(b_re, -1, -2).reshape(n, H, P)
    bT_im = jnp.swapaxes(b_im, -1, -2).reshape(n, H, P)
    a_re, a_im, bb_re, bb_im = pl.pallas_call(
        _s5_prep_kernel,
        out_shape=[jax.ShapeDtypeStruct((n, P), F32), jax.ShapeDtypeStruct((n, P), F32),
                   jax.ShapeDtypeStruct((n, H, P), F32), jax.ShapeDtypeStruct((n, H, P), F32)],
        name="s5_discretize",
    )(lam_re.reshape(n, P), lam_im.reshape(n, P), log_dt.reshape(n, 1), bT_re, bT_im)
    gh = G // 2
    gq = S5_PIECE // P
    nq = gh // gq
    pw = gq * P
    col = np.arange(pw)
    g_idx = np.arange(gh)[None, :, None, None]
    q_idx = np.arange(nq)[:, None, None, None]
    put_b = (col[None, None, None, :] == P * (g_idx - gq * q_idx) + np.arange(P)[None, None, :, None])
    put_b = jnp.asarray(put_b, BF16)
    bb = jnp.stack([bb_re, bb_im], axis=2).astype(BF16).reshape(depth, 2, 2, gh, H, 2, P)
    bm = jnp.einsum("ldkghcp,qgpn->ldkcqghn", bb, put_b).reshape(depth, 2, 2, 2 * nq, gh * H, pw)
    gg_idx = np.arange(gq)[None, :, None, None]
    put_c = (np.arange(gh * H)[None, None, None, :]
             == H * (gq * np.arange(nq)[:, None, None, None] + gg_idx) + np.arange(H)[None, None, :, None])
    put_c = jnp.asarray(put_c, BF16)
    cc = jnp.stack([c_re, -c_im], axis=4).astype(BF16).reshape(depth, 2, 2, nq, gq, H, 2, P)
    cm = jnp.einsum("ldkqghcp,qghn->ldkcqgpn", cc, put_c).reshape(depth, 2, 2, 2 * nq, pw, gh * H)
    a_re = jnp.tile(a_re.reshape(depth, 2, 2, gh * P), (1, 1, B, 1))
    a_im = jnp.tile(a_im.reshape(depth, 2, 2, gh * P), (1, 1, B, 1))
    return bm, cm, a_re, a_im


def _s5_kernel(uf0_ref, ufn_ref, ub_ref, bm_ref, cm_ref, are_ref, aim_ref, yf_ref, yb_ref,
               buf_f, buf_b, h_ref, yacc_ref, lhs_ref, *, B, n_chunks):
    T, pitch, shift = S5_T, S5_PITCH, S5_SHIFT
    R = 2 * B
    wide = T + SUBLANES
    hw = bm_ref.shape[3]
    npiece = bm_ref.shape[2]
    sw = are_ref.shape[2]
    nsl = sw // LANES
    per = S5_PIECE // LANES
    steps = T // (2 * npiece)
    s = pl.program_id(0)

    def window(b, k):
        r = 2 * b + k
        return (r * pitch - shift, wide) if k else (r * pitch, T)

    def put_lhs(u_ref):
        for k in range(2):
            parts = []
            for b in range(B):
                ub = u_ref[b, :, k * hw:(k + 1) * hw]
                if k:
                    ub = pltpu.roll(jnp.concatenate([ub, jnp.zeros((SUBLANES, hw), F32)], axis=0), shift, axis=0)
                parts.append(ub)
            rows = B * (wide if k else T)
            lhs_ref[k, 0:rows, :] = jnp.concatenate(parts, axis=0).astype(BF16)

    def bu_piece(buf, d, k, q):
        rows = wide if k else T
        bu = _dot(lhs_ref[k, 0:B * rows, :], bm_ref[d, k, q])
        for b in range(B):
            base, _ = window(b, k)
            for e in range(per):
                buf[per * q + e, base:base + rows, :] = bu[b * rows:(b + 1) * rows, e * LANES:(e + 1) * LANES]

    def readout_piece(buf, d, k, q):
        rows = wide if k else T
        parts = []
        for b in range(B):
            base, _ = window(b, k)
            parts.append(jnp.concatenate([buf[per * q + e, base:base + rows, :] for e in range(per)], axis=1))
        yacc_ref[k, 0:B * rows, :] += _dot(jnp.concatenate(parts, axis=0).astype(BF16), cm_ref[d, k, q])

    def scan_step(buf, d, t, h_re, h_im):
        idx = pl.ds(t, R, stride=pitch)
        b_re = jnp.concatenate([buf[sl, idx, :] for sl in range(nsl)], axis=1)
        b_im = jnp.concatenate([buf[nsl + sl, idx, :] for sl in range(nsl)], axis=1)
        a_re = are_ref[d]
        a_im = aim_ref[d]
        n_re = a_re * h_re - a_im * h_im + b_re
        n_im = a_re * h_im + a_im * h_re + b_im
        for sl in range(nsl):
            buf[sl, idx, :] = n_re[:, sl * LANES:(sl + 1) * LANES]
            buf[nsl + sl, idx, :] = n_im[:, sl * LANES:(sl + 1) * LANES]
        return n_re, n_im

    def slot(d_scan, buf_scan, d_mx, buf_mx, u_mx_ref, y_mx_ref):
        put_lhs(u_mx_ref)
        yacc_ref[...] = jnp.zeros_like(yacc_ref)
        carry = (h_ref[d_scan, 0], h_ref[d_scan, 1])
        for k in range(2):
            h_re, h_im = carry
            for q in range(npiece):
                readout_piece(buf_mx, d_mx, k, q)
                bu_piece(buf_mx, d_mx, k, q)
                for i in range(steps):
                    t = (k * npiece + q) * steps + i
                    h_re, h_im = scan_step(buf_scan, d_scan, T - 1 - t if d_scan else t, h_re, h_im)
            carry = (h_re, h_im)
            rows = wide if k else T
            for b in range(B):
                yb = yacc_ref[k, b * rows:(b + 1) * rows, :]
                if k:
                    yb = pltpu.roll(yb, rows - shift, axis=0)[:T]
                y_mx_ref[b, :, k * hw:(k + 1) * hw] = yb
        h_ref[d_scan, 0] = carry[0]
        h_ref[d_scan, 1] = carry[1]

    @pl.when(s == 0)
    def _():
        h_ref[...] = jnp.zeros_like(h_ref)
        buf_b[...] = jnp.zeros_like(buf_b)
        put_lhs(uf0_ref)
        for k in range(2):
            for q in range(npiece):
                bu_piece(buf_f, 0, k, q)

    slot(0, buf_f, 1, buf_b, ub_ref, yb_ref)

    @pl.when(s < n_chunks)
    def _():
        slot(1, buf_b, 0, buf_f, ufn_ref, yf_ref)


def _s5_scan(layer, su, bm, cm, a_re, a_im, dims):
    B, n_c, L = dims["B"], dims["n_c"], dims["L"]
    T = S5_T
    S = n_c + L
    ssm_w = su.shape[2]
    ncb, nlb = n_c // T, L // T
    R = 2 * B
    sw = a_re.shape[3]
    nslab = 2 * sw // LANES

    def bwd(s):
        return jnp.where(s < ncb, ncb - 1 - s, ncb + (nlb - 1 - (s - ncb)))

    blk = (B, T, ssm_w)
    nch = S // T
    last = nch - 1
    wide_rows = B * (T + SUBLANES)
    kern = functools.partial(_s5_kernel, B=B, n_chunks=nch)
    return pl.pallas_call(
        kern,
        grid=(nch + 1,),
        in_specs=[
            pl.BlockSpec(blk, lambda s: (0, 0, 0)),
            pl.BlockSpec(blk, lambda s: (0, jnp.minimum(s + 1, last), 0)),
            pl.BlockSpec(blk, lambda s: (0, bwd(jnp.minimum(s, last)), 0)),
            _resident((None,) + bm.shape[1:], lambda s: (layer, 0, 0, 0, 0, 0)),
            _resident((None,) + cm.shape[1:], lambda s: (layer, 0, 0, 0, 0, 0)),
            _resident((None,) + a_re.shape[1:], lambda s: (layer, 0, 0, 0)),
            _resident((None,) + a_im.shape[1:], lambda s: (layer, 0, 0, 0)),
        ],
        out_specs=[pl.BlockSpec(blk, lambda s: (0, jnp.minimum(s, last), 0)),
                   pl.BlockSpec(blk, lambda s: (0, bwd(jnp.maximum(s - 1, 0)), 0))],
        out_shape=[jax.ShapeDtypeStruct(su.shape, F32), jax.ShapeDtypeStruct(su.shape, F32)],
        scratch_shapes=[
            pltpu.VMEM((nslab, R * S5_PITCH, LANES), F32),
            pltpu.VMEM((nslab, R * S5_PITCH, LANES), F32),
            pltpu.VMEM((2, 2, R, sw), F32),
            pltpu.VMEM((2, wide_rows, S5_PIECE), F32),
            pltpu.VMEM((2, wide_rows, ssm_w // 2), BF16),
        ],
        compiler_params=_cparams(("arbitrary",)),
        name="s5_scan",
    )(su, su, su, bm, cm, a_re, a_im)


def _mix_out_kernel(*refs, n_x, n_ctx_tiles, off, chains):
    x_refs = refs[:n_x]
    oa_ref = refs[n_x]
    s5_refs = refs[n_x + 1:n_x + 1 + 3 * chains]
    (gm_ref, d_ref, wg_ref, bg_ref, lg_ref, lb_ref, ws_ref, bs_ref, wo_ref, gate_ref, gf_ref, shf_ref, scf_ref,
     o_ref, h_ref) = refs[n_x + 1 + 3 * chains:]
    sub = o_ref.shape[0] // chains
    gw = gm_ref.shape[1] // 2
    bs = bs_ref[...]
    for ch in range(chains):
        rows = slice(ch * sub, (ch + 1) * sub)
        yf_ref, yb_ref, su_ref = s5_refs[3 * ch:3 * ch + 3]
        y = d_ref[...] * su_ref[...] + yf_ref[...] + yb_ref[...]
        g1 = _gelu(y)
        o_ssm = g1 * _sigmoid(_dot(g1.astype(BF16), wg_ref[...]) + bg_ref[...])
        u = _gelu(gm_ref[rows, :gw])
        v = _gelu(gm_ref[rows, gw:])
        mu = jnp.mean(v, axis=-1, keepdims=True)
        var = jnp.mean(jnp.square(v - mu), axis=-1, keepdims=True)
        v = ((v - mu) * lax.rsqrt(var + NORM_EPS) * lg_ref[...] + lb_ref[...]).astype(BF16)
        mixed = []
        for c in range(sub // GMLP_CHUNK):
            cols = []
            for g in range(gw // GMLP_GROUP_W):
                vb = v[c * GMLP_CHUNK:(c + 1) * GMLP_CHUNK, g * GMLP_GROUP_W:(g + 1) * GMLP_GROUP_W]
                cols.append(_dot(ws_ref[g], vb) + bs[:, g:g + 1])
            mixed.append(jnp.concatenate(cols, axis=1))
        o_gmlp = u * jnp.concatenate(mixed, axis=0)
        mix = jnp.concatenate([oa_ref[rows, :], o_ssm.astype(BF16), o_gmlp.astype(BF16)], axis=1)
        x1 = _read_x(x_refs, n_ctx_tiles, off, rows) + gate_ref[...] * _dot(mix, wo_ref[...])
        o_ref[rows, :] = x1
        h_ref[rows, :] = _norm_mod(x1, gf_ref[...], shf_ref[...], scf_ref[...]).astype(BF16)


def _mix_out(layer, xs, o_attn, y_f, y_b, su, gm, p, g_ffn, mod4, dims, skip_ctx):
    B, n_c, L, ntok = dims["B"], dims["n_c"], dims["L"], dims["ntok"]
    d = xs[0].shape[1]
    sub = n_c
    lt = L // sub
    chains = 2 if (B % 2 == 0 and lt % 2 == 0) else 1
    tm = chains * sub
    attn_w, ssm_w, gm_w = o_attn.shape[1], su.shape[2], gm.shape[1]
    ngrp = p["w_s"].shape[1]
    nct = B * n_c // tm
    off = nct if skip_ctx else 0
    nrow = ntok - off * tm
    assert o_attn.shape[0] == nrow

    def row(i):
        return jnp.where(i + off < nct, B, (i + off - nct) // (L // tm))

    def seq(ch):
        def index(i):
            j = (i + off) * chains + ch
            return (jnp.where(j < B, j, (j - B) // lt), jnp.where(j < B, 0, 1 + (j - B) % lt), 0)
        return index

    vec = lambda w: pl.BlockSpec((None, 1, w), lambda i: (layer, 0, 0))
    modspec = lambda k: pl.BlockSpec((None, None, 1, d), lambda i: (layer, row(i), 0, k))
    s5_specs = [pl.BlockSpec((None, sub, ssm_w), seq(ch)) for ch in range(chains) for _ in range(3)]
    s5_args = [a for _ in range(chains) for a in (y_f, y_b, su)]
    kern = functools.partial(_mix_out_kernel, n_x=len(xs), n_ctx_tiles=nct, off=off, chains=chains)
    return pl.pallas_call(
        kern,
        grid=(nrow // tm,),
        in_specs=_x_specs(xs, tm, nct, off) + [pl.BlockSpec((tm, attn_w), lambda i: (i, 0))] + s5_specs + [
            pl.BlockSpec((tm, gm_w), lambda i: (i + off, 0)),
            vec(ssm_w),
            _resident((None, ssm_w, ssm_w), lambda i: (layer, 0, 0)),
            vec(ssm_w), vec(gm_w // 2), vec(gm_w // 2),
            _resident((None, ngrp, GMLP_CHUNK, GMLP_CHUNK), lambda i: (layer, 0, 0, 0)),
            pl.BlockSpec((None, GMLP_CHUNK, ngrp), lambda i: (layer, 0, 0)),
            _resident((None, d, d), lambda i: (layer, 0, 0)),
            modspec(2), vec(d), modspec(3), modspec(4),
        ],
        out_specs=[pl.BlockSpec((tm, d), lambda i: (i, 0)), pl.BlockSpec((tm, d), lambda i: (i, 0))],
        out_shape=[jax.ShapeDtypeStruct((nrow, d), F32), jax.ShapeDtypeStruct((nrow, d), BF16)],
        compiler_params=_cparams(("arbitrary",)),
        name="mix_out",
    )(*xs, o_attn, *s5_args, gm, p["ssm_d"], p["w_glu"], p["b_glu"], p["ln_g"], p["ln_b"],
      p["w_s"], p["b_s_t"], p["w_out"], mod4, g_ffn, mod4, mod4)


def _ffn_kernel(hm_ref, hp_ref, hn_ref, x_ref, gate_ref, gfin_ref, wu_ref, cw_ref, cb_ref, wd_ref,
                o_ref, h_ref, mp_ref, mn_ref, *, n_ctx_tiles, n_c, L, final):
    i = pl.program_id(0)
    j = pl.program_id(1)
    tm = x_ref.shape[0]
    halo = hp_ref.shape[0]

    @pl.when(j == 0)
    def _():
        h_ref[0:halo, :] = hp_ref[...]
        h_ref[halo:halo + tm, :] = hm_ref[...]
        h_ref[halo + tm:, :] = hn_ref[...]
        is_ctx = i < n_ctx_tiles
        seq_len = jnp.where(is_ctx, n_c, L)
        start = lax.rem(jnp.where(is_ctx, i * tm, (i - n_ctx_tiles) * tm), seq_len)
        pos = start + lax.broadcasted_iota(jnp.int32, (tm, LANES), 0)
        first = pos == 0
        last = pos == seq_len - 1
        for k in range(1, tm // min(n_c, L) + 1):
            first = jnp.logical_or(first, pos == k * seq_len)
            last = jnp.logical_or(last, pos == (k + 1) * seq_len - 1)
        mp_ref[...] = jnp.where(first, 0.0, 1.0)
        mn_ref[...] = jnp.where(last, 0.0, 1.0)
        o_ref[...] = jnp.zeros_like(o_ref)

    tf = wd_ref.shape[0]
    rep = tf // LANES
    ge = _dot(h_ref[...], wu_ref[:, 0:tf])
    val = _dot(h_ref[halo:halo + tm, :], wu_ref[:, tf:])
    ext = tm + 2 * halo
    g_prev = pltpu.roll(ge, 1, axis=0)[halo:halo + tm]
    g_next = pltpu.roll(ge, ext - 1, axis=0)[halo:halo + tm]
    mp = jnp.concatenate([mp_ref[...]] * rep, axis=1)
    mn = jnp.concatenate([mn_ref[...]] * rep, axis=1)
    cw = cw_ref[...]
    gc = cw[0:1] * (g_prev * mp) + cw[1:2] * ge[halo:halo + tm] + cw[2:3] * (g_next * mn) + cb_ref[...]
    act = (gc * _sigmoid(gc) * val).astype(BF16)
    o_ref[...] += _dot(act, wd_ref[...])

    @pl.when(j == pl.num_programs(1) - 1)
    def _():
        xn = x_ref[...] + gate_ref[...] * o_ref[...]
        if final:
            xn = xn * lax.rsqrt(jnp.mean(xn * xn, axis=-1, keepdims=True) + NORM_EPS) * gfin_ref[...]
        o_ref[...] = xn


def _conv_ffn(layer, x1, h2, mod4, w_up, conv_w, conv_b, w_down, g_final, dims, skip_ctx, final):
    B, n_c, L = dims["B"], dims["n_c"], dims["L"]
    nrow, d = x1.shape
    d_ff = w_down.shape[1]
    tm, tf = dims["ffn_tm"], FFN_TF
    halo = BF16_ROWS
    nf = d_ff // tf
    nct = 0 if skip_ctx else B * n_c // tm
    lt = L // tm
    nhalo = nrow // halo

    def row(i):
        return jnp.where(i < nct, B, (i - nct) // lt)

    modspec = lambda k: pl.BlockSpec((None, None, 1, d), lambda i, j: (layer, row(i), 0, k))
    kern = functools.partial(_ffn_kernel, n_ctx_tiles=nct, n_c=n_c, L=L, final=final)
    return pl.pallas_call(
        kern,
        grid=(nrow // tm, nf),
        in_specs=[
            pl.BlockSpec((tm, d), lambda i, j: (i, 0)),
            pl.BlockSpec((halo, d), lambda i, j: (jnp.maximum(i * (tm // halo) - 1, 0), 0)),
            pl.BlockSpec((halo, d), lambda i, j: (jnp.minimum((i + 1) * (tm // halo), nhalo - 1), 0)),
            pl.BlockSpec((tm, d), lambda i, j: (i, 0)),
            modspec(5),
            pl.BlockSpec((1, d), lambda i, j: (0, 0)),
            pl.BlockSpec((None, None, d, 2 * tf), lambda i, j: (layer, j, 0, 0)),
            pl.BlockSpec((None, 3, tf), lambda i, j: (layer, 0, j)),
            pl.BlockSpec((None, 1, tf), lambda i, j: (layer, 0, j)),
            pl.BlockSpec((None, tf, d), lambda i, j: (layer, j, 0)),
        ],
        out_specs=pl.BlockSpec((tm, d), lambda i, j: (i, 0)),
        out_shape=jax.ShapeDtypeStruct(x1.shape, F32),
        scratch_shapes=[
            pltpu.VMEM((tm + 2 * halo, d), BF16),
            pltpu.VMEM((tm, LANES), F32),
            pltpu.VMEM((tm, LANES), F32),
        ],
        compiler_params=_cparams(("arbitrary", "arbitrary")),
        name="conv_ffn",
    )(h2, h2, h2, x1, mod4, g_final, w_up, conv_w, conv_b, w_down)


def _rope_tables(n_c, L):
    n_freq = HEAD_DIM // 4
    t = np.arange(L)
    inv_freq = jnp.asarray(ROPE_BASE, F32) ** (-jnp.arange(n_freq, dtype=F32) / n_freq)
    pos = jnp.asarray(np.stack([t // GRID_W, t % GRID_W], axis=-1), F32)
    ang = pos[:, :, None] * inv_freq
    cos, sin = jnp.cos(ang), jnp.sin(ang)
    c64 = jnp.concatenate([cos, cos], axis=-1).reshape(L, HEAD_DIM)
    s64 = jnp.concatenate([-sin, sin], axis=-1).reshape(L, HEAD_DIM)
    reps = LANES // HEAD_DIM
    c = jnp.concatenate([jnp.ones((n_c, LANES), F32), jnp.tile(c64, (1, reps))], axis=0)
    s = jnp.concatenate([jnp.zeros((n_c, LANES), F32), jnp.tile(s64, (1, reps))], axis=0)
    return c, s


def kernel(x, c, ctx, c_ctx, w_ada, b_ada, g_mix, g_ffn, w_in, w_out, attn_sink, ssm_lambda_re, ssm_lambda_im,
           ssm_log_dt, ssm_b_re, ssm_b_im, ssm_c_re, ssm_c_im, ssm_d, ssm_w_glu, ssm_b_glu, gmlp_ln_g, gmlp_ln_b,
           gmlp_w_s, gmlp_b_s, ffn_w_up, ffn_conv_w, ffn_conv_b, ffn_w_down, g_final):
    B, L, d = x.shape
    n_c = ctx.shape[1]
    depth = w_in.shape[0]
    ssm_w = ssm_d.shape[1]
    gmlp_w = gmlp_ln_g.shape[1]
    attn_w = w_in.shape[2] - ssm_w - 2 * gmlp_w
    attn_w = attn_w * GQA_RATIO // (GQA_RATIO + 2)
    kv_w = attn_w // GQA_RATIO
    ntok = B * (n_c + L)
    ffn_tm = min(FFN_TM, B * n_c)
    assert n_c % ATTN_BLOCK == 0 and L % n_c == 0 and L % ffn_tm == 0 and (B * n_c) % ffn_tm == 0
    assert n_c % S5_T == 0 and L % S5_T == 0 and B + 1 <= SUBLANES
    dims = dict(B=B, n_c=n_c, L=L, ntok=ntok, attn_w=attn_w, kv_w=kv_w, ssm_w=ssm_w, ffn_tm=ffn_tm)

    cpad = jnp.zeros((SUBLANES, d), F32).at[:B].set(c).at[B].set(c_ctx)
    mod = _modulation(cpad, w_ada, b_ada)
    mod4 = mod.reshape(depth, SUBLANES, 1, N_MOD * d)

    bm, cm, a_re, a_im = _s5_prepare(ssm_lambda_re, ssm_lambda_im, ssm_log_dt, ssm_b_re, ssm_b_im,
                                     ssm_c_re, ssm_c_im, B)
    cos_t, sin_t = _rope_tables(n_c, L)
    bias = _attn_bias()

    w_in_b = w_in.astype(BF16)
    d_ff = ffn_w_down.shape[1]
    nf = d_ff // FFN_TF
    w_up_b = jnp.transpose(ffn_w_up.reshape(depth, d, 2, nf, FFN_TF), (0, 3, 1, 2, 4)).astype(BF16)
    w_up_b = w_up_b.reshape(depth, nf, d, 2 * FFN_TF)
    w_down_b = ffn_w_down.astype(BF16)
    mixp = dict(
        ssm_d=ssm_d.reshape(depth, 1, ssm_w), w_glu=ssm_w_glu.astype(BF16), b_glu=ssm_b_glu.reshape(depth, 1, ssm_w),
        ln_g=gmlp_ln_g.reshape(depth, 1, gmlp_w), ln_b=gmlp_ln_b.reshape(depth, 1, gmlp_w),
        w_s=gmlp_w_s.astype(BF16), b_s_t=jnp.swapaxes(gmlp_b_s, 1, 2), w_out=w_out.astype(BF16))
    g_mix3 = g_mix.reshape(depth, 1, d)
    g_ffn3 = g_ffn.reshape(depth, 1, d)
    conv_b3 = ffn_conv_b.reshape(depth, 1, -1)

    xs = (ctx.reshape(B * n_c, d), x.reshape(B * L, d))
    g_fin = g_final.reshape(1, d)
    for layer in range(depth):
        last = layer == depth - 1
        q, kx, vx, su, gm = _in_proj(layer, xs, g_mix3, mod4, w_in_b, cos_t, sin_t, dims)
        o_attn = _attention(layer, q, kx, vx, attn_sink, bias, dims, skip_ctx=last)
        y_f, y_b = _s5_scan(layer, su, bm, cm, a_re, a_im, dims)
        x1, h2 = _mix_out(layer, xs, o_attn, y_f, y_b, su, gm, mixp, g_ffn3, mod4, dims, skip_ctx=last)
        xs = (_conv_ffn(layer, x1, h2, mod4, w_up_b, ffn_conv_w, conv_b3, w_down_b, g_fin, dims,
                        skip_ctx=last, final=last),)
    return xs[0].reshape(B, L, d)
```

```python
import functools
import math

import numpy as np
import jax
import jax.numpy as jnp
from jax import lax
from jax.experimental import pallas as pl
from jax.experimental.pallas import tpu as pltpu

F32 = jnp.float32
BF16 = jnp.bfloat16

HEAD_DIM = 64
GQA_RATIO = 8
GRID_W = 64
ROPE_BASE = 10000.0
ATTN_BLOCK = 128
SSM_GROUP = 16
SSM_STATE = 64
GMLP_CHUNK = 128
GMLP_GROUP_W = 128
N_MOD = 6
NORM_EPS = 1e-6
MASK_NEG = -1e30
LOG2E = 1.4426950408889634

LANES = 128
SUBLANES = 8
BF16_ROWS = 16

MOD_TN = 1024
FFN_TM = 512
FFN_TF = 512
ATTN_STACK = 2
S5_T = 128
S5_PITCH = S5_T + 4
S5_SHIFT = 4
S5_PIECE = 256
VMEM_LIMIT = 56 * 1024 * 1024


def _cparams(sem):
    return pltpu.CompilerParams(dimension_semantics=sem, vmem_limit_bytes=VMEM_LIMIT)


def _sigmoid(x):
    return 1.0 / (1.0 + jnp.exp(-x))


def _gelu(x):
    k1 = -2.0 * 0.7978845608028654 * LOG2E
    e = jnp.exp2(x * (k1 + (k1 * 0.044715) * (x * x)))
    return x / (1.0 + e)


def _norm_mod(x, g, shift, scale):
    y = x * lax.rsqrt(jnp.mean(x * x, axis=-1, keepdims=True) + NORM_EPS)
    return y * (g * (1.0 + scale)) + shift


def _dot(a, b):
    return jnp.dot(a, b, preferred_element_type=F32)


def _resident(block_shape, index_map):
    return pl.BlockSpec(block_shape, index_map, pipeline_mode=pl.Buffered(1))


def _mod_kernel(c_ref, w_ref, b_ref, o_ref):
    c = c_ref[...]
    act = (c * _sigmoid(c)).astype(BF16)
    o_ref[...] = _dot(act, w_ref[...].astype(BF16)) + b_ref[...]


def _modulation(cpad, w_ada, b_ada):
    depth, d, n = w_ada.shape
    rows = cpad.shape[0]
    return pl.pallas_call(
        _mod_kernel,
        grid=(depth, n // MOD_TN),
        in_specs=[
            pl.BlockSpec((rows, d), lambda l, j: (0, 0)),
            pl.BlockSpec((None, d, MOD_TN), lambda l, j: (l, 0, j)),
            pl.BlockSpec((None, 1, MOD_TN), lambda l, j: (l, 0, j)),
        ],
        out_specs=pl.BlockSpec((None, rows, MOD_TN), lambda l, j: (l, 0, j)),
        out_shape=jax.ShapeDtypeStruct((depth, rows, n), F32),
        compiler_params=_cparams(("arbitrary", "arbitrary")),
        name="adaln_mod",
    )(cpad, w_ada, b_ada.reshape(depth, 1, n))


def _rope(x, c, s, lane):
    first = (lane % 32) < 16
    partner = jnp.where(first, pltpu.roll(x, LANES - 16, axis=1), pltpu.roll(x, 16, axis=1))
    return x * c + partner * s


def _x_specs(xs, tm, n_ctx_tiles, off=0):
    d = xs[0].shape[1]
    if len(xs) == 1:
        return [pl.BlockSpec((tm, d), lambda i: (i + off, 0))]
    return [pl.BlockSpec((tm, d), lambda i: (jnp.minimum(i + off, n_ctx_tiles - 1), 0)),
            pl.BlockSpec((tm, d), lambda i: (jnp.maximum(i + off - n_ctx_tiles, 0), 0))]


def _read_x(x_refs, n_ctx_tiles, off=0, rows=slice(None)):
    if len(x_refs) == 1:
        return x_refs[0][rows, :]
    return jnp.where(pl.program_id(0) + off < n_ctx_tiles, x_refs[0][rows, :], x_refs[1][rows, :])


def _in_proj_kernel(*refs, attn_w, kv_w, ssm_w, n_x, n_ctx_tiles):
    x = _read_x(refs[:n_x], n_ctx_tiles)
    g_ref, sh_ref, sc_ref, w_ref, cos_ref, sin_ref, q_ref, kx_ref, vx_ref, su_ref, gm_ref = refs[n_x:]
    h = _norm_mod(x, g_ref[...], sh_ref[...], sc_ref[...]).astype(BF16)
    z = _dot(h, w_ref[...])
    tm = z.shape[0]
    c = cos_ref[...]
    s = sin_ref[...]
    lane = lax.broadcasted_iota(jnp.int32, (tm, LANES), 1)
    qscale = HEAD_DIM ** -0.5 * LOG2E
    for j in range(attn_w // LANES):
        zq = z[:, j * LANES:(j + 1) * LANES]
        q_ref[:, j * LANES:(j + 1) * LANES] = (_rope(zq, c, s, lane) * qscale).astype(BF16)
    low = lane < HEAD_DIM
    off_k, off_v = attn_w, attn_w + kv_w
    for src, dst, rot in ((z[:, off_k:off_k + kv_w], kx_ref, True), (z[:, off_v:off_v + kv_w], vx_ref, False)):
        t = _rope(src, c, s, lane) if rot else src
        h0 = jnp.where(low, t, 0.0)
        h1 = jnp.where(low, 0.0, t)
        parts = (h0, pltpu.roll(h0, HEAD_DIM, axis=1), pltpu.roll(h1, HEAD_DIM, axis=1), h1)
        for p, val in enumerate(parts):
            dst[:, p * LANES:(p + 1) * LANES] = val.astype(BF16)
    off_s = attn_w + 2 * kv_w
    su_ref[...] = z[:, off_s:off_s + ssm_w]
    gm_ref[...] = z[:, off_s + ssm_w:]


def _in_proj(layer, xs, g_mix, mod4, w_in, cos_t, sin_t, dims):
    B, n_c, L, ntok = dims["B"], dims["n_c"], dims["L"], dims["ntok"]
    d = xs[0].shape[1]
    n_in = w_in.shape[2]
    attn_w, kv_w, ssm_w = dims["attn_w"], dims["kv_w"], dims["ssm_w"]
    gm_w = n_in - attn_w - 2 * kv_w - ssm_w
    tm = n_c
    lt = L // tm
    S = n_c + L

    def row(i):
        return jnp.where(i < B, B, (i - B) // lt)

    def tbl(i):
        return jnp.where(i < B, 0, 1 + (i - B) % lt)

    def seq(i):
        return (jnp.where(i < B, i, (i - B) // lt), jnp.where(i < B, 0, 1 + (i - B) % lt), 0)

    kern = functools.partial(_in_proj_kernel, attn_w=attn_w, kv_w=kv_w, ssm_w=ssm_w, n_x=len(xs), n_ctx_tiles=B)
    return pl.pallas_call(
        kern,
        grid=(ntok // tm,),
        in_specs=_x_specs(xs, tm, B) + [
            pl.BlockSpec((None, 1, d), lambda i: (layer, 0, 0)),
            pl.BlockSpec((None, None, 1, d), lambda i: (layer, row(i), 0, 0)),
            pl.BlockSpec((None, None, 1, d), lambda i: (layer, row(i), 0, 1)),
            _resident((None, d, n_in), lambda i: (layer, 0, 0)),
            pl.BlockSpec((tm, LANES), lambda i: (tbl(i), 0)),
            pl.BlockSpec((tm, LANES), lambda i: (tbl(i), 0)),
        ],
        out_specs=[
            pl.BlockSpec((tm, attn_w), lambda i: (i, 0)),
            pl.BlockSpec((tm, 4 * LANES), lambda i: (i, 0)),
            pl.BlockSpec((tm, 4 * LANES), lambda i: (i, 0)),
            pl.BlockSpec((None, tm, ssm_w), seq),
            pl.BlockSpec((tm, gm_w), lambda i: (i, 0)),
        ],
        out_shape=[
            jax.ShapeDtypeStruct((ntok, attn_w), BF16),
            jax.ShapeDtypeStruct((ntok, 4 * LANES), BF16),
            jax.ShapeDtypeStruct((ntok, 4 * LANES), BF16),
            jax.ShapeDtypeStruct((B, S, ssm_w), F32),
            jax.ShapeDtypeStruct((ntok, gm_w), F32),
        ],
        compiler_params=_cparams(("arbitrary",)),
        name="in_proj",
    )(*xs, g_mix, mod4, mod4, w_in, cos_t, sin_t)


def _attn_kernel(sink_ref, q_ref, kp_ref, ko_ref, kn_ref, kc_ref, vp_ref, vo_ref, vn_ref, vc_ref,
                 bias_ref, o_ref, s_ref, p_ref, *, layer, n_kv):
    blk = q_ref.shape[0]
    per_kv = GQA_RATIO * HEAD_DIM // LANES
    bias = bias_ref[...]
    nwin = bias.shape[1] // LANES
    nkeys = bias.shape[1] + kc_ref.shape[0]
    low = lax.broadcasted_iota(jnp.int32, (blk, LANES), 1) < HEAD_DIM
    krefs = (kp_ref, ko_ref, kn_ref, kc_ref)
    vrefs = (vp_ref, vo_ref, vn_ref, vc_ref)
    def keys(refs, g, half):
        lo = (2 * g + half) * LANES
        return [r[:, lo:lo + LANES] for r in refs]

    chain_ids = [(g, c0) for g in range(n_kv) for c0 in range(0, per_kv, ATTN_STACK)]
    for ch, (g, c0) in enumerate(chain_ids):
        kb = jnp.concatenate(keys(krefs, g, 0) + keys(krefs, g, 1), axis=0)
        qs = jnp.concatenate([q_ref[:, (g * per_kv + c0 + c) * LANES:(g * per_kv + c0 + c + 1) * LANES]
                              for c in range(ATTN_STACK)], axis=0)
        s_ref[ch] = lax.dot_general(qs, kb, (((1,), (1,)), ((), ())), preferred_element_type=F32)
    for ch, (g, c0) in enumerate(chain_ids):
        vb = jnp.concatenate(keys(vrefs, g, 0) + keys(vrefs, g, 1), axis=0)
        scales = []
        for c in range(ATTN_STACK):
            rows = slice(c * blk, (c + 1) * blk)
            invs = []
            for p in range(2):
                sink = sink_ref[layer, g * GQA_RATIO + 2 * (c0 + c) + p] * LOG2E
                slabs = [slice(p * nkeys + k * LANES, p * nkeys + (k + 1) * LANES) for k in range(nkeys // LANES)]
                def scores(k, sl):
                    sk = s_ref[ch, rows, sl]
                    return sk + bias[:, k * LANES:(k + 1) * LANES] if k < nwin else sk
                mx = None
                for k, sl in enumerate(slabs):
                    sk = scores(k, sl)
                    mx = sk if mx is None else jnp.maximum(mx, sk)
                m = jnp.maximum(jnp.max(mx, axis=1, keepdims=True), sink)
                acc = None
                for k, sl in enumerate(slabs):
                    e = jnp.exp2(scores(k, sl) - m)
                    p_ref[ch, rows, sl] = e.astype(BF16)
                    acc = e if acc is None else acc + e
                den = jnp.sum(acc, axis=1, keepdims=True) + jnp.exp2(sink - m)
                invs.append(1.0 / den)
            scales.append(jnp.where(low, invs[0], invs[1]))
        o = _dot(p_ref[ch], vb)
        for c in range(ATTN_STACK):
            col = (g * per_kv + c0 + c) * LANES
            o_ref[:, col:col + LANES] = (o[c * blk:(c + 1) * blk] * scales[c]).astype(o_ref.dtype)


def _attn_bias():
    blk = ATTN_BLOCK
    r = np.arange(blk)[:, None]
    j = np.arange(blk)[None, :]
    zero = np.zeros((blk, blk), np.float32)
    neg = np.full((blk, blk), MASK_NEG, np.float32)
    prev = np.where(j >= r, 0.0, MASK_NEG).astype(np.float32)
    nxt = np.where(j <= r, 0.0, MASK_NEG).astype(np.float32)
    variants = [
        np.concatenate([neg, zero, nxt], 1),
        np.concatenate([prev, zero, nxt], 1),
        np.concatenate([prev, zero, neg], 1),
        np.concatenate([neg, neg, neg], 1),
    ]
    return jnp.asarray(np.stack(variants))


def _attention(layer, q, kx, vx, sink, bias, dims, skip_ctx):
    B, n_c, L, ntok = dims["B"], dims["n_c"], dims["L"], dims["ntok"]
    blk = ATTN_BLOCK
    attn_w = q.shape[1]
    n_kv = dims["kv_w"] // HEAD_DIM
    nblk = ntok // blk
    nctb = B * n_c // blk
    cpb = n_c // blk
    nb = L // blk
    assert nb >= 2
    off = nctb if skip_ctx else 0

    def bat(t):
        return jnp.where(t < nctb, t // cpb, (t - nctb) // nb)

    def variant(t):
        n = (t - nctb) % nb
        return jnp.where(t < nctb, 3, jnp.where(n == 0, 0, jnp.where(n == nb - 1, 2, 1)))

    kvw = kx.shape[1]
    own = pl.BlockSpec((blk, kvw), lambda t: (t + off, 0))
    prev = pl.BlockSpec((blk, kvw), lambda t: (jnp.maximum(t + off - 1, 0), 0))
    nxt = pl.BlockSpec((blk, kvw), lambda t: (jnp.minimum(t + off + 1, nblk - 1), 0))
    ctx = pl.BlockSpec((n_c, kvw), lambda t: (bat(t + off), 0))
    kern = functools.partial(_attn_kernel, layer=layer, n_kv=n_kv)
    return pl.pallas_call(
        kern,
        grid=(nblk - off,),
        in_specs=[
            pl.BlockSpec(memory_space=pltpu.SMEM),
            pl.BlockSpec((blk, attn_w), lambda t: (t + off, 0)),
            prev, own, nxt, ctx, prev, own, nxt, ctx,
            pl.BlockSpec((None,) + bias.shape[1:], lambda t: (variant(t + off), 0, 0)),
        ],
        out_specs=pl.BlockSpec((blk, attn_w), lambda t: (t, 0)),
        out_shape=jax.ShapeDtypeStruct(((nblk - off) * blk, attn_w), BF16),
        scratch_shapes=[
            pltpu.VMEM((attn_w // LANES // ATTN_STACK, ATTN_STACK * blk, 2 * (bias.shape[2] + n_c)), F32),
            pltpu.VMEM((attn_w // LANES // ATTN_STACK, ATTN_STACK * blk, 2 * (bias.shape[2] + n_c)), BF16),
        ],
        compiler_params=_cparams(("arbitrary",)),
        name="window_attn",
    )(sink, q, kx, kx, kx, kx, vx, vx, vx, vx, bias)


def _s5_prep_kernel(lre_ref, lim_ref, ldt_ref, bre_ref, bim_ref, are_ref, aim_ref, bbre_ref, bbim_ref):
    lre = lre_ref[...]
    lim = lim_ref[...]
    dt = jnp.exp(ldt_ref[...])
    mag = jnp.exp(lre * dt)
    a_re = mag * jnp.cos(lim * dt)
    a_im = mag * jnp.sin(lim * dt)
    den = lre * lre + lim * lim
    n_re = a_re - 1.0
    f_re = ((n_re * lre + a_im * lim) / den)[:, None, :]
    f_im = ((a_im * lre - n_re * lim) / den)[:, None, :]
    are_ref[...] = a_re
    aim_ref[...] = a_im
    bre = bre_ref[...]
    bim = bim_ref[...]
    bbre_ref[...] = f_re * bre - f_im * bim
    bbim_ref[...] = f_re * bim + f_im * bre


def _s5_prepare(lam_re, lam_im, log_dt, b_re, b_im, c_re, c_im, B):
    depth, _, G, P = lam_re.shape
    H = b_re.shape[-1]
    n = depth * 2 * G
    bT_re = jnp.swapaxes(b_re, -1, -2).reshape(n, H, P)
    bT_im = jnp.swapaxes(b_im, -1, -2).reshape(n, H, P)
    a_re, a_im, bb_re, bb_im = pl.pallas_call(
        _s5_prep_kernel,
        out_shape=[jax.ShapeDtypeStruct((n, P), F32), jax.ShapeDtypeStruct((n, P), F32),
                   jax.ShapeDtypeStruct((n, H, P), F32), jax.ShapeDtypeStruct((n, H, P), F32)],
        name="s5_discretize",
    )(lam_re.reshape(n, P), lam_im.reshape(n, P), log_dt.reshape(n, 1), bT_re, bT_im)
    gh = G // 2
    gq = S5_PIECE // P
    nq = gh // gq
    pw = gq * P
    col = np.arange(pw)
    g_idx = np.arange(gh)[None, :, None, None]
    q_idx = np.arange(nq)[:, None, None, None]
    put_b = (col[None, None, None, :] == P * (g_idx - gq * q_idx) + np.arange(P)[None, None, :, None])
    put_b = jnp.asarray(put_b, BF16)
    bb = jnp.stack([bb_re, bb_im], axis=2).astype(BF16).reshape(depth, 2, 2, gh, H, 2, P)
    bm = jnp.einsum("ldkghcp,qgpn->ldkcqghn", bb, put_b).reshape(depth, 2, 2, 2 * nq, gh * H, pw)
    gg_idx = np.arange(gq)[None, :, None, None]
    put_c = (np.arange(gh * H)[None, None, None, :]
             == H * (gq * np.arange(nq)[:, None, None, None] + gg_idx) + np.arange(H)[None, None, :, None])
    put_c = jnp.asarray(put_c, BF16)
    cc = jnp.stack([c_re, -c_im], axis=4).astype(BF16).reshape(depth, 2, 2, nq, gq, H, 2, P)
    cm = jnp.einsum("ldkqghcp,qghn->ldkcqgpn", cc, put_c).reshape(depth, 2, 2, 2 * nq, pw, gh * H)
    a_re = jnp.tile(a_re.reshape(depth, 2, 2, gh * P), (1, 1, B, 1))
    a_im = jnp.tile(a_im.reshape(depth, 2, 2, gh * P), (1, 1, B, 1))
    return bm, cm, a_re, a_im


def _s5_kernel(uf0_ref, ufn_ref, ub_ref, bm_ref, cm_ref, are_ref, aim_ref, yf_ref, yb_ref,
               buf_f, buf_b, h_ref, yacc_ref, lhs_ref, *, B, n_chunks):
    T, pitch, shift = S5_T, S5_PITCH, S5_SHIFT
    R = 2 * B
    wide = T + SUBLANES
    hw = bm_ref.shape[3]
    npiece = bm_ref.shape[2]
    sw = are_ref.shape[2]
    nsl = sw // LANES
    per = S5_PIECE // LANES
    steps = T // (2 * npiece)
    s = pl.program_id(0)

    def window(b, k):
        r = 2 * b + k
        return (r * pitch - shift, wide) if k else (r * pitch, T)

    def put_lhs(u_ref):
        for k in range(2):
            parts = []
            for b in range(B):
                ub = u_ref[b, :, k * hw:(k + 1) * hw]
                if k:
                    ub = pltpu.roll(jnp.concatenate([ub, jnp.zeros((SUBLANES, hw), F32)], axis=0), shift, axis=0)
                parts.append(ub)
            rows = B * (wide if k else T)
            lhs_ref[k, 0:rows, :] = jnp.concatenate(parts, axis=0).astype(BF16)

    def bu_piece(buf, d, k, q):
        rows = wide if k else T
        bu = _dot(lhs_ref[k, 0:B * rows, :], bm_ref[d, k, q])
        for b in range(B):
            base, _ = window(b, k)
            for e in range(per):
                buf[per * q + e, base:base + rows, :] = bu[b * rows:(b + 1) * rows, e * LANES:(e + 1) * LANES]

    def readout_piece(buf, d, k, q):
        rows = wide if k else T
        parts = []
        for b in range(B):
            base, _ = window(b, k)
            parts.append(jnp.concatenate([buf[per * q + e, base:base + rows, :] for e in range(per)], axis=1))
        yacc_ref[k, 0:B * rows, :] += _dot(jnp.concatenate(parts, axis=0).astype(BF16), cm_ref[d, k, q])

    def scan_step(buf, d, t, h_re, h_im):
        idx = pl.ds(t, R, stride=pitch)
        b_re = jnp.concatenate([buf[sl, idx, :] for sl in range(nsl)], axis=1)
        b_im = jnp.concatenate([buf[nsl + sl, idx, :] for sl in range(nsl)], axis=1)
        a_re = are_ref[d]
        a_im = aim_ref[d]
        n_re = a_re * h_re - a_im * h_im + b_re
        n_im = a_re * h_im + a_im * h_re + b_im
        for sl in range(nsl):
            buf[sl, idx, :] = n_re[:, sl * LANES:(sl + 1) * LANES]
            buf[nsl + sl, idx, :] = n_im[:, sl * LANES:(sl + 1) * LANES]
        return n_re, n_im

    def slot(d_scan, buf_scan, d_mx, buf_mx, u_mx_ref, y_mx_ref):
        put_lhs(u_mx_ref)
        yacc_ref[...] = jnp.zeros_like(yacc_ref)
        carry = (h_ref[d_scan, 0], h_ref[d_scan, 1])
        for k in range(2):
            h_re, h_im = carry
            for q in range(npiece):
                readout_piece(buf_mx, d_mx, k, q)
                bu_piece(buf_mx, d_mx, k, q)
                for i in range(steps):
                    t = (k * npiece + q) * steps + i
                    h_re, h_im = scan_step(buf_scan, d_scan, T - 1 - t if d_scan else t, h_re, h_im)
            carry = (h_re, h_im)
            rows = wide if k else T
            for b in range(B):
                yb = yacc_ref[k, b * rows:(b + 1) * rows, :]
                if k:
                    yb = pltpu.roll(yb, rows - shift, axis=0)[:T]
                y_mx_ref[b, :, k * hw:(k + 1) * hw] = yb
        h_ref[d_scan, 0] = carry[0]
        h_ref[d_scan, 1] = carry[1]

    @pl.when(s == 0)
    def _():
        h_ref[...] = jnp.zeros_like(h_ref)
        buf_b[...] = jnp.zeros_like(buf_b)
        put_lhs(uf0_ref)
        for k in range(2):
            for q in range(npiece):
                bu_piece(buf_f, 0, k, q)

    slot(0, buf_f, 1, buf_b, ub_ref, yb_ref)

    @pl.when(s < n_chunks)
    def _():
        slot(1, buf_b, 0, buf_f, ufn_ref, yf_ref)


def _s5_scan(layer, su, bm, cm, a_re, a_im, dims):
    B, n_c, L = dims["B"], dims["n_c"], dims["L"]
    T = S5_T
    S = n_c + L
    ssm_w = su.shape[2]
    ncb, nlb = n_c // T, L // T
    R = 2 * B
    sw = a_re.shape[3]
    nslab = 2 * sw // LANES

    def bwd(s):
        return jnp.where(s < ncb, ncb - 1 - s, ncb + (nlb - 1 - (s - ncb)))

    blk = (B, T, ssm_w)
    nch = S // T
    last = nch - 1
    wide_rows = B * (T + SUBLANES)
    kern = functools.partial(_s5_kernel, B=B, n_chunks=nch)
    return pl.pallas_call(
        kern,
        grid=(nch + 1,),
        in_specs=[
            pl.BlockSpec(blk, lambda s: (0, 0, 0)),
            pl.BlockSpec(blk, lambda s: (0, jnp.minimum(s + 1, last), 0)),
            pl.BlockSpec(blk, lambda s: (0, bwd(jnp.minimum(s, last)), 0)),
            _resident((None,) + bm.shape[1:], lambda s: (layer, 0, 0, 0, 0, 0)),
            _resident((None,) + cm.shape[1:], lambda s: (layer, 0, 0, 0, 0, 0)),
            _resident((None,) + a_re.shape[1:], lambda s: (layer, 0, 0, 0)),
            _resident((None,) + a_im.shape[1:], lambda s: (layer, 0, 0, 0)),
        ],
        out_specs=[pl.BlockSpec(blk, lambda s: (0, jnp.minimum(s, last), 0)),
                   pl.BlockSpec(blk, lambda s: (0, bwd(jnp.maximum(s - 1, 0)), 0))],
        out_shape=[jax.ShapeDtypeStruct(su.shape, F32), jax.ShapeDtypeStruct(su.shape, F32)],
        scratch_shapes=[
            pltpu.VMEM((nslab, R * S5_PITCH, LANES), F32),
            pltpu.VMEM((nslab, R * S5_PITCH, LANES), F32),
            pltpu.VMEM((2, 2, R, sw), F32),
            pltpu.VMEM((2, wide_rows, S5_PIECE), F32),
            pltpu.VMEM((2, wide_rows, ssm_w // 2), BF16),
        ],
        compiler_params=_cparams(("arbitrary",)),
        name="s5_scan",
    )(su, su, su, bm, cm, a_re, a_im)


def _mix_out_kernel(*refs, n_x, n_ctx_tiles, off, chains):
    x_refs = refs[:n_x]
    oa_ref = refs[n_x]
    s5_refs = refs[n_x + 1:n_x + 1 + 3 * chains]
    (gm_ref, d_ref, wg_ref, bg_ref, lg_ref, lb_ref, ws_ref, bs_ref, wo_ref, gate_ref, gf_ref, shf_ref, scf_ref,
     o_ref, h_ref) = refs[n_x + 1 + 3 * chains:]
    sub = o_ref.shape[0] // chains
    gw = gm_ref.shape[1] // 2
    bs = bs_ref[...]
    for ch in range(chains):
        rows = slice(ch * sub, (ch + 1) * sub)
        yf_ref, yb_ref, su_ref = s5_refs[3 * ch:3 * ch + 3]
        y = d_ref[...] * su_ref[...] + yf_ref[...] + yb_ref[...]
        g1 = _gelu(y)
        o_ssm = g1 * _sigmoid(_dot(g1.astype(BF16), wg_ref[...]) + bg_ref[...])
        u = _gelu(gm_ref[rows, :gw])
        v = _gelu(gm_ref[rows, gw:])
        mu = jnp.mean(v, axis=-1, keepdims=True)
        var = jnp.mean(jnp.square(v - mu), axis=-1, keepdims=True)
        v = ((v - mu) * lax.rsqrt(var + NORM_EPS) * lg_ref[...] + lb_ref[...]).astype(BF16)
        mixed = []
        for c in range(sub // GMLP_CHUNK):
            cols = []
            for g in range(gw // GMLP_GROUP_W):
                vb = v[c * GMLP_CHUNK:(c + 1) * GMLP_CHUNK, g * GMLP_GROUP_W:(g + 1) * GMLP_GROUP_W]
                cols.append(_dot(ws_ref[g], vb) + bs[:, g:g + 1])
            mixed.append(jnp.concatenate(cols, axis=1))
        o_gmlp = u * jnp.concatenate(mixed, axis=0)
        mix = jnp.concatenate([oa_ref[rows, :], o_ssm.astype(BF16), o_gmlp.astype(BF16)], axis=1)
        x1 = _read_x(x_refs, n_ctx_tiles, off, rows) + gate_ref[...] * _dot(mix, wo_ref[...])
        o_ref[rows, :] = x1
        h_ref[rows, :] = _norm_mod(x1, gf_ref[...], shf_ref[...], scf_ref[...]).astype(BF16)


def _mix_out(layer, xs, o_attn, y_f, y_b, su, gm, p, g_ffn, mod4, dims, skip_ctx):
    B, n_c, L, ntok = dims["B"], dims["n_c"], dims["L"], dims["ntok"]
    d = xs[0].shape[1]
    sub = n_c
    lt = L // sub
    chains = 2 if (B % 2 == 0 and lt % 2 == 0) else 1
    tm = chains * sub
    attn_w, ssm_w, gm_w = o_attn.shape[1], su.shape[2], gm.shape[1]
    ngrp = p["w_s"].shape[1]
    nct = B * n_c // tm
    off = nct if skip_ctx else 0
    nrow = ntok - off * tm
    assert o_attn.shape[0] == nrow

    def row(i):
        return jnp.where(i + off < nct, B, (i + off - nct) // (L // tm))

    def seq(ch):
        def index(i):
            j = (i + off) * chains + ch
            return (jnp.where(j < B, j, (j - B) // lt), jnp.where(j < B, 0, 1 + (j - B) % lt), 0)
        return index

    vec = lambda w: pl.BlockSpec((None, 1, w), lambda i: (layer, 0, 0))
    modspec = lambda k: pl.BlockSpec((None, None, 1, d), lambda i: (layer, row(i), 0, k))
    s5_specs = [pl.BlockSpec((None, sub, ssm_w), seq(ch)) for ch in range(chains) for _ in range(3)]
    s5_args = [a for _ in range(chains) for a in (y_f, y_b, su)]
    kern = functools.partial(_mix_out_kernel, n_x=len(xs), n_ctx_tiles=nct, off=off, chains=chains)
    return pl.pallas_call(
        kern,
        grid=(nrow // tm,),
        in_specs=_x_specs(xs, tm, nct, off) + [pl.BlockSpec((tm, attn_w), lambda i: (i, 0))] + s5_specs + [
            pl.BlockSpec((tm, gm_w), lambda i: (i + off, 0)),
            vec(ssm_w),
            _resident((None, ssm_w, ssm_w), lambda i: (layer, 0, 0)),
            vec(ssm_w), vec(gm_w // 2), vec(gm_w // 2),
            _resident((None, ngrp, GMLP_CHUNK, GMLP_CHUNK), lambda i: (layer, 0, 0, 0)),
            pl.BlockSpec((None, GMLP_CHUNK, ngrp), lambda i: (layer, 0, 0)),
            _resident((None, d, d), lambda i: (layer, 0, 0)),
            modspec(2), vec(d), modspec(3), modspec(4),
        ],
        out_specs=[pl.BlockSpec((tm, d), lambda i: (i, 0)), pl.BlockSpec((tm, d), lambda i: (i, 0))],
        out_shape=[jax.ShapeDtypeStruct((nrow, d), F32), jax.ShapeDtypeStruct((nrow, d), BF16)],
        compiler_params=_cparams(("arbitrary",)),
        name="mix_out",
    )(*xs, o_attn, *s5_args, gm, p["ssm_d"], p["w_glu"], p["b_glu"], p["ln_g"], p["ln_b"],
      p["w_s"], p["b_s_t"], p["w_out"], mod4, g_ffn, mod4, mod4)


def _pack_w_up_kernel(g_ref, v_ref, o_ref):
    tf = g_ref.shape[1]
    o_ref[:, 0:tf] = g_ref[...].astype(BF16)
    o_ref[:, tf:] = v_ref[...].astype(BF16)


def _pack_w_up(w_up, d_ff):
    depth, d, _ = w_up.shape
    tf = FFN_TF
    nf = d_ff // tf
    return pl.pallas_call(
        _pack_w_up_kernel,
        grid=(depth, nf),
        in_specs=[pl.BlockSpec((None, d, tf), lambda l, j: (l, 0, j)),
                  pl.BlockSpec((None, d, tf), lambda l, j: (l, 0, nf + j))],
        out_specs=pl.BlockSpec((None, None, d, 2 * tf), lambda l, j: (l, j, 0, 0)),
        out_shape=jax.ShapeDtypeStruct((depth, nf, d, 2 * tf), BF16),
        compiler_params=_cparams(("arbitrary", "arbitrary")),
        name="pack_w_up",
    )(w_up, w_up)


def _ffn_kernel(hm_ref, hp_ref, hn_ref, x_ref, gate_ref, gfin_ref, wu_ref, cw_ref, cb_ref, wd_ref,
                o_ref, h_ref, mp_ref, mn_ref, *, n_ctx_tiles, n_c, L, final):
    i = pl.program_id(0)
    j = pl.program_id(1)
    tm = x_ref.shape[0]
    halo = hp_ref.shape[0]

    @pl.when(j == 0)
    def _():
        h_ref[0:halo, :] = hp_ref[...]
        h_ref[halo:halo + tm, :] = hm_ref[...]
        h_ref[halo + tm:, :] = hn_ref[...]
        is_ctx = i < n_ctx_tiles
        seq_len = jnp.where(is_ctx, n_c, L)
        start = lax.rem(jnp.where(is_ctx, i * tm, (i - n_ctx_tiles) * tm), seq_len)
        pos = start + lax.broadcasted_iota(jnp.int32, (tm, LANES), 0)
        first = pos == 0
        last = pos == seq_len - 1
        for k in range(1, tm // min(n_c, L) + 1):
            first = jnp.logical_or(first, pos == k * seq_len)
            last = jnp.logical_or(last, pos == (k + 1) * seq_len - 1)
        mp_ref[...] = jnp.where(first, 0.0, 1.0)
        mn_ref[...] = jnp.where(last, 0.0, 1.0)
        o_ref[...] = jnp.zeros_like(o_ref)

    tf = wd_ref.shape[0]
    rep = tf // LANES
    ge = _dot(h_ref[...], wu_ref[:, 0:tf])
    val = _dot(h_ref[halo:halo + tm, :], wu_ref[:, tf:])
    ext = tm + 2 * halo
    g_prev = pltpu.roll(ge, 1, axis=0)[halo:halo + tm]
    g_next = pltpu.roll(ge, ext - 1, axis=0)[halo:halo + tm]
    mp = jnp.concatenate([mp_ref[...]] * rep, axis=1)
    mn = jnp.concatenate([mn_ref[...]] * rep, axis=1)
    cw = cw_ref[...]
    gc = cw[0:1] * (g_prev * mp) + cw[1:2] * ge[halo:halo + tm] + cw[2:3] * (g_next * mn) + cb_ref[...]
    act = (gc * _sigmoid(gc) * val).astype(BF16)
    o_ref[...] += _dot(act, wd_ref[...])

    @pl.when(j == pl.num_programs(1) - 1)
    def _():
        xn = x_ref[...] + gate_ref[...] * o_ref[...]
        if final:
            xn = xn * lax.rsqrt(jnp.mean(xn * xn, axis=-1, keepdims=True) + NORM_EPS) * gfin_ref[...]
        o_ref[...] = xn


def _conv_ffn(layer, x1, h2, mod4, w_up, conv_w, conv_b, w_down, g_final, dims, skip_ctx, final):
    B, n_c, L = dims["B"], dims["n_c"], dims["L"]
    nrow, d = x1.shape
    d_ff = w_down.shape[1]
    tm, tf = dims["ffn_tm"], FFN_TF
    halo = BF16_ROWS
    nf = d_ff // tf
    nct = 0 if skip_ctx else B * n_c // tm
    lt = L // tm
    nhalo = nrow // halo

    def row(i):
        return jnp.where(i < nct, B, (i - nct) // lt)

    modspec = lambda k: pl.BlockSpec((None, None, 1, d), lambda i, j: (layer, row(i), 0, k))
    kern = functools.partial(_ffn_kernel, n_ctx_tiles=nct, n_c=n_c, L=L, final=final)
    return pl.pallas_call(
        kern,
        grid=(nrow // tm, nf),
        in_specs=[
            pl.BlockSpec((tm, d), lambda i, j: (i, 0)),
            pl.BlockSpec((halo, d), lambda i, j: (jnp.maximum(i * (tm // halo) - 1, 0), 0)),
            pl.BlockSpec((halo, d), lambda i, j: (jnp.minimum((i + 1) * (tm // halo), nhalo - 1), 0)),
            pl.BlockSpec((tm, d), lambda i, j: (i, 0)),
            modspec(5),
            pl.BlockSpec((1, d), lambda i, j: (0, 0)),
            pl.BlockSpec((None, None, d, 2 * tf), lambda i, j: (layer, j, 0, 0)),
            pl.BlockSpec((None, 3, tf), lambda i, j: (layer, 0, j)),
            pl.BlockSpec((None, 1, tf), lambda i, j: (layer, 0, j)),
            pl.BlockSpec((None, tf, d), lambda i, j: (layer, j, 0)),
        ],
        out_specs=pl.BlockSpec((tm, d), lambda i, j: (i, 0)),
        out_shape=jax.ShapeDtypeStruct(x1.shape, F32),
        scratch_shapes=[
            pltpu.VMEM((tm + 2 * halo, d), BF16),
            pltpu.VMEM((tm, LANES), F32),
            pltpu.VMEM((tm, LANES), F32),
        ],
        compiler_params=_cparams(("arbitrary", "arbitrary")),
        name="conv_ffn",
    )(h2, h2, h2, x1, mod4, g_final, w_up, conv_w, conv_b, w_down)


def _rope_tables(n_c, L):
    n_freq = HEAD_DIM // 4
    t = np.arange(L)
    inv_freq = jnp.asarray(ROPE_BASE, F32) ** (-jnp.arange(n_freq, dtype=F32) / n_freq)
    pos = jnp.asarray(np.stack([t // GRID_W, t % GRID_W], axis=-1), F32)
    ang = pos[:, :, None] * inv_freq
    cos, sin = jnp.cos(ang), jnp.sin(ang)
    c64 = jnp.concatenate([cos, cos], axis=-1).reshape(L, HEAD_DIM)
    s64 = jnp.concatenate([-sin, sin], axis=-1).reshape(L, HEAD_DIM)
    reps = LANES // HEAD_DIM
    c = jnp.concatenate([jnp.ones((n_c, LANES), F32), jnp.tile(c64, (1, reps))], axis=0)
    s = jnp.concatenate([jnp.zeros((n_c, LANES), F32), jnp.tile(s64, (1, reps))], axis=0)
    return c, s


def kernel(x, c, ctx, c_ctx, w_ada, b_ada, g_mix, g_ffn, w_in, w_out, attn_sink, ssm_lambda_re, ssm_lambda_im,
           ssm_log_dt, ssm_b_re, ssm_b_im, ssm_c_re, ssm_c_im, ssm_d, ssm_w_glu, ssm_b_glu, gmlp_ln_g, gmlp_ln_b,
           gmlp_w_s, gmlp_b_s, ffn_w_up, ffn_conv_w, ffn_conv_b, ffn_w_down, g_final):
    B, L, d = x.shape
    n_c = ctx.shape[1]
    depth = w_in.shape[0]
    ssm_w = ssm_d.shape[1]
    gmlp_w = gmlp_ln_g.shape[1]
    attn_w = w_in.shape[2] - ssm_w - 2 * gmlp_w
    attn_w = attn_w * GQA_RATIO // (GQA_RATIO + 2)
    kv_w = attn_w // GQA_RATIO
    ntok = B * (n_c + L)
    ffn_tm = min(FFN_TM, B * n_c)
    assert n_c % ATTN_BLOCK == 0 and L % n_c == 0 and L % ffn_tm == 0 and (B * n_c) % ffn_tm == 0
    assert n_c % S5_T == 0 and L % S5_T == 0 and B + 1 <= SUBLANES
    dims = dict(B=B, n_c=n_c, L=L, ntok=ntok, attn_w=attn_w, kv_w=kv_w, ssm_w=ssm_w, ffn_tm=ffn_tm)

    cpad = jnp.zeros((SUBLANES, d), F32).at[:B].set(c).at[B].set(c_ctx)
    mod = _modulation(cpad, w_ada, b_ada)
    mod4 = mod.reshape(depth, SUBLANES, 1, N_MOD * d)

    bm, cm, a_re, a_im = _s5_prepare(ssm_lambda_re, ssm_lambda_im, ssm_log_dt, ssm_b_re, ssm_b_im,
                                     ssm_c_re, ssm_c_im, B)
    cos_t, sin_t = _rope_tables(n_c, L)
    bias = _attn_bias()

    w_in_b = w_in.astype(BF16)
    w_up_b = _pack_w_up(ffn_w_up, ffn_w_down.shape[1])
    w_down_b = ffn_w_down.astype(BF16)
    mixp = dict(
        ssm_d=ssm_d.reshape(depth, 1, ssm_w), w_glu=ssm_w_glu.astype(BF16), b_glu=ssm_b_glu.reshape(depth, 1, ssm_w),
        ln_g=gmlp_ln_g.reshape(depth, 1, gmlp_w), ln_b=gmlp_ln_b.reshape(depth, 1, gmlp_w),
        w_s=gmlp_w_s.astype(BF16), b_s_t=jnp.swapaxes(gmlp_b_s, 1, 2), w_out=w_out.astype(BF16))
    g_mix3 = g_mix.reshape(depth, 1, d)
    g_ffn3 = g_ffn.reshape(depth, 1, d)
    conv_b3 = ffn_conv_b.reshape(depth, 1, -1)

    xs = (ctx.reshape(B * n_c, d), x.reshape(B * L, d))
    g_fin = g_final.reshape(1, d)
    for layer in range(depth):
        last = layer == depth - 1
        q, kx, vx, su, gm = _in_proj(layer, xs, g_mix3, mod4, w_in_b, cos_t, sin_t, dims)
        o_attn = _attention(layer, q, kx, vx, attn_sink, bias, dims, skip_ctx=last)
        y_f, y_b = _s5_scan(layer, su, bm, cm, a_re, a_im, dims)
        x1, h2 = _mix_out(layer, xs, o_attn, y_f, y_b, su, gm, mixp, g_ffn3, mod4, dims, skip_ctx=last)
        xs = (_conv_ffn(layer, x1, h2, mod4, w_up_b, ffn_conv_w, conv_b3, w_down_b, g_fin, dims,
                        skip_ctx=last, final=last),)
    return xs[0].reshape(B, L, d)
```

```python
import functools
import math

import numpy as np
import jax
import jax.numpy as jnp
from jax import lax
from jax.experimental import pallas as pl
from jax.experimental.pallas import tpu as pltpu

F32 = jnp.float32
BF16 = jnp.bfloat16

HEAD_DIM = 64
GQA_RATIO = 8
GRID_W = 64
ROPE_BASE = 10000.0
ATTN_BLOCK = 128
SSM_GROUP = 16
SSM_STATE = 64
GMLP_CHUNK = 128
GMLP_GROUP_W = 128
N_MOD = 6
NORM_EPS = 1e-6
MASK_NEG = -1e30
LOG2E = 1.4426950408889634

LANES = 128
SUBLANES = 8
BF16_ROWS = 16

MOD_TN = 1024
FFN_TM = 1024
FFN_TF = 512
ATTN_STACK = 2
S5_T = 128
S5_PITCH = S5_T + 4
S5_SHIFT = 4
S5_PIECE = 256
VMEM_LIMIT = 60 * 1024 * 1024


def _cparams(sem):
    return pltpu.CompilerParams(dimension_semantics=sem, vmem_limit_bytes=VMEM_LIMIT)


def _sigmoid(x):
    return 1.0 / (1.0 + jnp.exp(-x))


def _gelu(x):
    k1 = -2.0 * 0.7978845608028654 * LOG2E
    e = jnp.exp2(x * (k1 + (k1 * 0.044715) * (x * x)))
    return x / (1.0 + e)


def _norm_mod(x, g, shift, scale):
    y = x * lax.rsqrt(jnp.mean(x * x, axis=-1, keepdims=True) + NORM_EPS)
    return y * (g * (1.0 + scale)) + shift


def _dot(a, b):
    return jnp.dot(a, b, preferred_element_type=F32)


def _resident(block_shape, index_map):
    return pl.BlockSpec(block_shape, index_map, pipeline_mode=pl.Buffered(1))


def _mod_kernel(c_ref, w_ref, b_ref, o_ref):
    c = c_ref[...]
    act = (c * _sigmoid(c)).astype(BF16)
    o_ref[...] = _dot(act, w_ref[...].astype(BF16)) + b_ref[...]


def _modulation(cpad, w_ada, b_ada):
    depth, d, n = w_ada.shape
    rows = cpad.shape[0]
    return pl.pallas_call(
        _mod_kernel,
        grid=(depth, n // MOD_TN),
        in_specs=[
            pl.BlockSpec((rows, d), lambda l, j: (0, 0)),
            pl.BlockSpec((None, d, MOD_TN), lambda l, j: (l, 0, j)),
            pl.BlockSpec((None, 1, MOD_TN), lambda l, j: (l, 0, j)),
        ],
        out_specs=pl.BlockSpec((None, rows, MOD_TN), lambda l, j: (l, 0, j)),
        out_shape=jax.ShapeDtypeStruct((depth, rows, n), F32),
        compiler_params=_cparams(("arbitrary", "arbitrary")),
        name="adaln_mod",
    )(cpad, w_ada, b_ada.reshape(depth, 1, n))


def _rope(x, c, s, lane):
    first = (lane % 32) < 16
    partner = jnp.where(first, pltpu.roll(x, LANES - 16, axis=1), pltpu.roll(x, 16, axis=1))
    return x * c + partner * s


def _x_specs(xs, tm, n_ctx_tiles, off=0):
    d = xs[0].shape[1]
    if len(xs) == 1:
        return [pl.BlockSpec((tm, d), lambda i: (i + off, 0))]
    return [pl.BlockSpec((tm, d), lambda i: (jnp.minimum(i + off, n_ctx_tiles - 1), 0)),
            pl.BlockSpec((tm, d), lambda i: (jnp.maximum(i + off - n_ctx_tiles, 0), 0))]


def _read_x(x_refs, n_ctx_tiles, off=0, rows=slice(None)):
    if len(x_refs) == 1:
        return x_refs[0][rows, :]
    return jnp.where(pl.program_id(0) + off < n_ctx_tiles, x_refs[0][rows, :], x_refs[1][rows, :])


def _in_proj_kernel(*refs, attn_w, kv_w, ssm_w, n_x, n_ctx_tiles):
    x = _read_x(refs[:n_x], n_ctx_tiles)
    g_ref, sh_ref, sc_ref, w_ref, cos_ref, sin_ref, q_ref, kx_ref, vx_ref, su_ref, gm_ref = refs[n_x:]
    h = _norm_mod(x, g_ref[...], sh_ref[...], sc_ref[...]).astype(BF16)
    z = _dot(h, w_ref[...])
    tm = z.shape[0]
    c = cos_ref[...]
    s = sin_ref[...]
    lane = lax.broadcasted_iota(jnp.int32, (tm, LANES), 1)
    qscale = HEAD_DIM ** -0.5 * LOG2E
    for j in range(attn_w // LANES):
        zq = z[:, j * LANES:(j + 1) * LANES]
        q_ref[:, j * LANES:(j + 1) * LANES] = (_rope(zq, c, s, lane) * qscale).astype(BF16)
    low = lane < HEAD_DIM
    off_k, off_v = attn_w, attn_w + kv_w
    for src, dst, rot in ((z[:, off_k:off_k + kv_w], kx_ref, True), (z[:, off_v:off_v + kv_w], vx_ref, False)):
        t = _rope(src, c, s, lane) if rot else src
        h0 = jnp.where(low, t, 0.0)
        h1 = jnp.where(low, 0.0, t)
        parts = (h0, pltpu.roll(h0, HEAD_DIM, axis=1), pltpu.roll(h1, HEAD_DIM, axis=1), h1)
        for p, val in enumerate(parts):
            dst[:, p * LANES:(p + 1) * LANES] = val.astype(BF16)
    off_s = attn_w + 2 * kv_w
    su_ref[...] = z[:, off_s:off_s + ssm_w]
    gm_ref[...] = z[:, off_s + ssm_w:]


def _in_proj(layer, xs, g_mix, mod4, w_in, cos_t, sin_t, dims):
    B, n_c, L, ntok = dims["B"], dims["n_c"], dims["L"], dims["ntok"]
    d = xs[0].shape[1]
    n_in = w_in.shape[2]
    attn_w, kv_w, ssm_w = dims["attn_w"], dims["kv_w"], dims["ssm_w"]
    gm_w = n_in - attn_w - 2 * kv_w - ssm_w
    tm = n_c
    lt = L // tm
    S = n_c + L

    def row(i):
        return jnp.where(i < B, B, (i - B) // lt)

    def tbl(i):
        return jnp.where(i < B, 0, 1 + (i - B) % lt)

    def seq(i):
        return (jnp.where(i < B, i, (i - B) // lt), jnp.where(i < B, 0, 1 + (i - B) % lt), 0)

    kern = functools.partial(_in_proj_kernel, attn_w=attn_w, kv_w=kv_w, ssm_w=ssm_w, n_x=len(xs), n_ctx_tiles=B)
    return pl.pallas_call(
        kern,
        grid=(ntok // tm,),
        in_specs=_x_specs(xs, tm, B) + [
            pl.BlockSpec((None, 1, d), lambda i: (layer, 0, 0)),
            pl.BlockSpec((None, None, 1, d), lambda i: (layer, row(i), 0, 0)),
            pl.BlockSpec((None, None, 1, d), lambda i: (layer, row(i), 0, 1)),
            _resident((None, d, n_in), lambda i: (layer, 0, 0)),
            pl.BlockSpec((tm, LANES), lambda i: (tbl(i), 0)),
            pl.BlockSpec((tm, LANES), lambda i: (tbl(i), 0)),
        ],
        out_specs=[
            pl.BlockSpec((tm, attn_w), lambda i: (i, 0)),
            pl.BlockSpec((tm, 4 * LANES), lambda i: (i, 0)),
            pl.BlockSpec((tm, 4 * LANES), lambda i: (i, 0)),
            pl.BlockSpec((None, tm, ssm_w), seq),
            pl.BlockSpec((tm, gm_w), lambda i: (i, 0)),
        ],
        out_shape=[
            jax.ShapeDtypeStruct((ntok, attn_w), BF16),
            jax.ShapeDtypeStruct((ntok, 4 * LANES), BF16),
            jax.ShapeDtypeStruct((ntok, 4 * LANES), BF16),
            jax.ShapeDtypeStruct((B, S, ssm_w), F32),
            jax.ShapeDtypeStruct((ntok, gm_w), F32),
        ],
        compiler_params=_cparams(("arbitrary",)),
        name="in_proj",
    )(*xs, g_mix, mod4, mod4, w_in, cos_t, sin_t)


def _attn_kernel(sink_ref, q_ref, kp_ref, ko_ref, kn_ref, kc_ref, vp_ref, vo_ref, vn_ref, vc_ref,
                 bias_ref, o_ref, s_ref, p_ref, *, layer, n_kv):
    blk = q_ref.shape[0]
    per_kv = GQA_RATIO * HEAD_DIM // LANES
    bias = bias_ref[...]
    nwin = bias.shape[1] // LANES
    nkeys = bias.shape[1] + kc_ref.shape[0]
    low = lax.broadcasted_iota(jnp.int32, (blk, LANES), 1) < HEAD_DIM
    krefs = (kp_ref, ko_ref, kn_ref, kc_ref)
    vrefs = (vp_ref, vo_ref, vn_ref, vc_ref)
    def keys(refs, g, half):
        lo = (2 * g + half) * LANES
        return [r[:, lo:lo + LANES] for r in refs]

    chain_ids = [(g, c0) for g in range(n_kv) for c0 in range(0, per_kv, ATTN_STACK)]
    for ch, (g, c0) in enumerate(chain_ids):
        kb = jnp.concatenate(keys(krefs, g, 0) + keys(krefs, g, 1), axis=0)
        qs = jnp.concatenate([q_ref[:, (g * per_kv + c0 + c) * LANES:(g * per_kv + c0 + c + 1) * LANES]
                              for c in range(ATTN_STACK)], axis=0)
        s_ref[ch] = lax.dot_general(qs, kb, (((1,), (1,)), ((), ())), preferred_element_type=F32)
    for ch, (g, c0) in enumerate(chain_ids):
        vb = jnp.concatenate(keys(vrefs, g, 0) + keys(vrefs, g, 1), axis=0)
        scales = []
        for c in range(ATTN_STACK):
            rows = slice(c * blk, (c + 1) * blk)
            invs = []
            for p in range(2):
                sink = sink_ref[layer, g * GQA_RATIO + 2 * (c0 + c) + p] * LOG2E
                slabs = [slice(p * nkeys + k * LANES, p * nkeys + (k + 1) * LANES) for k in range(nkeys // LANES)]
                def scores(k, sl):
                    sk = s_ref[ch, rows, sl]
                    return sk + bias[:, k * LANES:(k + 1) * LANES] if k < nwin else sk
                mx = None
                for k, sl in enumerate(slabs):
                    sk = scores(k, sl)
                    mx = sk if mx is None else jnp.maximum(mx, sk)
                m = jnp.maximum(jnp.max(mx, axis=1, keepdims=True), sink)
                acc = None
                for k, sl in enumerate(slabs):
                    e = jnp.exp2(scores(k, sl) - m)
                    p_ref[ch, rows, sl] = e.astype(BF16)
                    acc = e if acc is None else acc + e
                den = jnp.sum(acc, axis=1, keepdims=True) + jnp.exp2(sink - m)
                invs.append(1.0 / den)
            scales.append(jnp.where(low, invs[0], invs[1]))
        o = _dot(p_ref[ch], vb)
        for c in range(ATTN_STACK):
            col = (g * per_kv + c0 + c) * LANES
            o_ref[:, col:col + LANES] = (o[c * blk:(c + 1) * blk] * scales[c]).astype(o_ref.dtype)


def _attn_bias():
    blk = ATTN_BLOCK
    r = np.arange(blk)[:, None]
    j = np.arange(blk)[None, :]
    zero = np.zeros((blk, blk), np.float32)
    neg = np.full((blk, blk), MASK_NEG, np.float32)
    prev = np.where(j >= r, 0.0, MASK_NEG).astype(np.float32)
    nxt = np.where(j <= r, 0.0, MASK_NEG).astype(np.float32)
    variants = [
        np.concatenate([neg, zero, nxt], 1),
        np.concatenate([prev, zero, nxt], 1),
        np.concatenate([prev, zero, neg], 1),
        np.concatenate([neg, neg, neg], 1),
    ]
    return jnp.asarray(np.stack(variants))


def _attention(layer, q, kx, vx, sink, bias, dims, skip_ctx):
    B, n_c, L, ntok = dims["B"], dims["n_c"], dims["L"], dims["ntok"]
    blk = ATTN_BLOCK
    attn_w = q.shape[1]
    n_kv = dims["kv_w"] // HEAD_DIM
    nblk = ntok // blk
    nctb = B * n_c // blk
    cpb = n_c // blk
    nb = L // blk
    assert nb >= 2
    off = nctb if skip_ctx else 0

    def bat(t):
        return jnp.where(t < nctb, t // cpb, (t - nctb) // nb)

    def variant(t):
        n = (t - nctb) % nb
        return jnp.where(t < nctb, 3, jnp.where(n == 0, 0, jnp.where(n == nb - 1, 2, 1)))

    kvw = kx.shape[1]
    own = pl.BlockSpec((blk, kvw), lambda t: (t + off, 0))
    prev = pl.BlockSpec((blk, kvw), lambda t: (jnp.maximum(t + off - 1, 0), 0))
    nxt = pl.BlockSpec((blk, kvw), lambda t: (jnp.minimum(t + off + 1, nblk - 1), 0))
    ctx = pl.BlockSpec((n_c, kvw), lambda t: (bat(t + off), 0))
    kern = functools.partial(_attn_kernel, layer=layer, n_kv=n_kv)
    return pl.pallas_call(
        kern,
        grid=(nblk - off,),
        in_specs=[
            pl.BlockSpec(memory_space=pltpu.SMEM),
            pl.BlockSpec((blk, attn_w), lambda t: (t + off, 0)),
            prev, own, nxt, ctx, prev, own, nxt, ctx,
            pl.BlockSpec((None,) + bias.shape[1:], lambda t: (variant(t + off), 0, 0)),
        ],
        out_specs=pl.BlockSpec((blk, attn_w), lambda t: (t, 0)),
        out_shape=jax.ShapeDtypeStruct(((nblk - off) * blk, attn_w), BF16),
        scratch_shapes=[
            pltpu.VMEM((attn_w // LANES // ATTN_STACK, ATTN_STACK * blk, 2 * (bias.shape[2] + n_c)), F32),
            pltpu.VMEM((attn_w // LANES // ATTN_STACK, ATTN_STACK * blk, 2 * (bias.shape[2] + n_c)), BF16),
        ],
        compiler_params=_cparams(("arbitrary",)),
        name="window_attn",
    )(sink, q, kx, kx, kx, kx, vx, vx, vx, vx, bias)


def _s5_prep_kernel(lre_ref, lim_ref, ldt_ref, bre_ref, bim_ref, are_ref, aim_ref, bbre_ref, bbim_ref):
    lre = lre_ref[...]
    lim = lim_ref[...]
    dt = jnp.exp(ldt_ref[...])
    mag = jnp.exp(lre * dt)
    a_re = mag * jnp.cos(lim * dt)
    a_im = mag * jnp.sin(lim * dt)
    den = lre * lre + lim * lim
    n_re = a_re - 1.0
    f_re = ((n_re * lre + a_im * lim) / den)[:, None, :]
    f_im = ((a_im * lre - n_re * lim) / den)[:, None, :]
    are_ref[...] = a_re
    aim_ref[...] = a_im
    bre = bre_ref[...]
    bim = bim_ref[...]
    bbre_ref[...] = f_re * bre - f_im * bim
    bbim_ref[...] = f_re * bim + f_im * bre


def _s5_prepare(lam_re, lam_im, log_dt, b_re, b_im, c_re, c_im, B):
    depth, _, G, P = lam_re.shape
    H = b_re.shape[-1]
    n = depth * 2 * G
    bT_re = jnp.swapaxes(b_re, -1, -2).reshape(n, H, P)
    bT_im = jnp.swapaxes(b_im, -1, -2).reshape(n, H, P)
    a_re, a_im, bb_re, bb_im = pl.pallas_call(
        _s5_prep_kernel,
        out_shape=[jax.ShapeDtypeStruct((n, P), F32), jax.ShapeDtypeStruct((n, P), F32),
                   jax.ShapeDtypeStruct((n, H, P), F32), jax.ShapeDtypeStruct((n, H, P), F32)],
        name="s5_discretize",
    )(lam_re.reshape(n, P), lam_im.reshape(n, P), log_dt.reshape(n, 1), bT_re, bT_im)
    gh = G // 2
    gq = S5_PIECE // P
    nq = gh // gq
    pw = gq * P
    col = np.arange(pw)
    g_idx = np.arange(gh)[None, :, None, None]
    q_idx = np.arange(nq)[:, None, None, None]
    put_b = (col[None, None, None, :] == P * (g_idx - gq * q_idx) + np.arange(P)[None, None, :, None])
    put_b = jnp.asarray(put_b, BF16)
    bb = jnp.stack([bb_re, bb_im], axis=2).astype(BF16).reshape(depth, 2, 2, gh, H, 2, P)
    bm = jnp.einsum("ldkghcp,qgpn->ldkcqghn", bb, put_b).reshape(depth, 2, 2, 2 * nq, gh * H, pw)
    gg_idx = np.arange(gq)[None, :, None, None]
    put_c = (np.arange(gh * H)[None, None, None, :]
             == H * (gq * np.arange(nq)[:, None, None, None] + gg_idx) + np.arange(H)[None, None, :, None])
    put_c = jnp.asarray(put_c, BF16)
    cc = jnp.stack([c_re, -c_im], axis=4).astype(BF16).reshape(depth, 2, 2, nq, gq, H, 2, P)
    cm = jnp.einsum("ldkqghcp,qghn->ldkcqgpn", cc, put_c).reshape(depth, 2, 2, 2 * nq, pw, gh * H)
    a_re = jnp.tile(a_re.reshape(depth, 2, 2, gh * P), (1, 1, B, 1))
    a_im = jnp.tile(a_im.reshape(depth, 2, 2, gh * P), (1, 1, B, 1))
    return bm, cm, a_re, a_im


def _s5_kernel(uf0_ref, ufn_ref, ub_ref, bm_ref, cm_ref, are_ref, aim_ref, yf_ref, yb_ref,
               buf_f, buf_b, h_ref, yacc_ref, lhs_ref, *, B, n_chunks):
    T, pitch, shift = S5_T, S5_PITCH, S5_SHIFT
    R = 2 * B
    wide = T + SUBLANES
    hw = bm_ref.shape[3]
    npiece = bm_ref.shape[2]
    sw = are_ref.shape[2]
    nsl = sw // LANES
    per = S5_PIECE // LANES
    steps = T // (2 * npiece)
    s = pl.program_id(0)

    def window(b, k):
        r = 2 * b + k
        return (r * pitch - shift, wide) if k else (r * pitch, T)

    def put_lhs(u_ref):
        for k in range(2):
            parts = []
            for b in range(B):
                ub = u_ref[b, :, k * hw:(k + 1) * hw]
                if k:
                    ub = pltpu.roll(jnp.concatenate([ub, jnp.zeros((SUBLANES, hw), F32)], axis=0), shift, axis=0)
                parts.append(ub)
            rows = B * (wide if k else T)
            lhs_ref[k, 0:rows, :] = jnp.concatenate(parts, axis=0).astype(BF16)

    def bu_piece(buf, d, k, q):
        rows = wide if k else T
        bu = _dot(lhs_ref[k, 0:B * rows, :], bm_ref[d, k, q])
        for b in range(B):
            base, _ = window(b, k)
            for e in range(per):
                buf[per * q + e, base:base + rows, :] = bu[b * rows:(b + 1) * rows, e * LANES:(e + 1) * LANES]

    def readout_piece(buf, d, k, q):
        rows = wide if k else T
        parts = []
        for b in range(B):
            base, _ = window(b, k)
            parts.append(jnp.concatenate([buf[per * q + e, base:base + rows, :] for e in range(per)], axis=1))
        yacc_ref[k, 0:B * rows, :] += _dot(jnp.concatenate(parts, axis=0).astype(BF16), cm_ref[d, k, q])

    def scan_step(buf, d, t, h_re, h_im):
        idx = pl.ds(t, R, stride=pitch)
        b_re = jnp.concatenate([buf[sl, idx, :] for sl in range(nsl)], axis=1)
        b_im = jnp.concatenate([buf[nsl + sl, idx, :] for sl in range(nsl)], axis=1)
        a_re = are_ref[d]
        a_im = aim_ref[d]
        n_re = a_re * h_re - a_im * h_im + b_re
        n_im = a_re * h_im + a_im * h_re + b_im
        for sl in range(nsl):
            buf[sl, idx, :] = n_re[:, sl * LANES:(sl + 1) * LANES]
            buf[nsl + sl, idx, :] = n_im[:, sl * LANES:(sl + 1) * LANES]
        return n_re, n_im

    def slot(d_scan, buf_scan, d_mx, buf_mx, u_mx_ref, y_mx_ref):
        put_lhs(u_mx_ref)
        yacc_ref[...] = jnp.zeros_like(yacc_ref)
        carry = (h_ref[d_scan, 0], h_ref[d_scan, 1])
        for k in range(2):
            h_re, h_im = carry
            for q in range(npiece):
                readout_piece(buf_mx, d_mx, k, q)
                bu_piece(buf_mx, d_mx, k, q)
                for i in range(steps):
                    t = (k * npiece + q) * steps + i
                    h_re, h_im = scan_step(buf_scan, d_scan, T - 1 - t if d_scan else t, h_re, h_im)
            carry = (h_re, h_im)
            rows = wide if k else T
            for b in range(B):
                yb = yacc_ref[k, b * rows:(b + 1) * rows, :]
                if k:
                    yb = pltpu.roll(yb, rows - shift, axis=0)[:T]
                y_mx_ref[b, :, k * hw:(k + 1) * hw] = yb
        h_ref[d_scan, 0] = carry[0]
        h_ref[d_scan, 1] = carry[1]

    @pl.when(s == 0)
    def _():
        h_ref[...] = jnp.zeros_like(h_ref)
        buf_b[...] = jnp.zeros_like(buf_b)
        put_lhs(uf0_ref)
        for k in range(2):
            for q in range(npiece):
                bu_piece(buf_f, 0, k, q)

    slot(0, buf_f, 1, buf_b, ub_ref, yb_ref)

    @pl.when(s < n_chunks)
    def _():
        slot(1, buf_b, 0, buf_f, ufn_ref, yf_ref)


def _s5_scan(layer, su, bm, cm, a_re, a_im, dims):
    B, n_c, L = dims["B"], dims["n_c"], dims["L"]
    T = S5_T
    S = n_c + L
    ssm_w = su.shape[2]
    ncb, nlb = n_c // T, L // T
    R = 2 * B
    sw = a_re.shape[3]
    nslab = 2 * sw // LANES

    def bwd(s):
        return jnp.where(s < ncb, ncb - 1 - s, ncb + (nlb - 1 - (s - ncb)))

    blk = (B, T, ssm_w)
    nch = S // T
    last = nch - 1
    wide_rows = B * (T + SUBLANES)
    kern = functools.partial(_s5_kernel, B=B, n_chunks=nch)
    return pl.pallas_call(
        kern,
        grid=(nch + 1,),
        in_specs=[
            pl.BlockSpec(blk, lambda s: (0, 0, 0)),
            pl.BlockSpec(blk, lambda s: (0, jnp.minimum(s + 1, last), 0)),
            pl.BlockSpec(blk, lambda s: (0, bwd(jnp.minimum(s, last)), 0)),
            _resident((None,) + bm.shape[1:], lambda s: (layer, 0, 0, 0, 0, 0)),
            _resident((None,) + cm.shape[1:], lambda s: (layer, 0, 0, 0, 0, 0)),
            _resident((None,) + a_re.shape[1:], lambda s: (layer, 0, 0, 0)),
            _resident((None,) + a_im.shape[1:], lambda s: (layer, 0, 0, 0)),
        ],
        out_specs=[pl.BlockSpec(blk, lambda s: (0, jnp.minimum(s, last), 0)),
                   pl.BlockSpec(blk, lambda s: (0, bwd(jnp.maximum(s - 1, 0)), 0))],
        out_shape=[jax.ShapeDtypeStruct(su.shape, F32), jax.ShapeDtypeStruct(su.shape, F32)],
        scratch_shapes=[
            pltpu.VMEM((nslab, R * S5_PITCH, LANES), F32),
            pltpu.VMEM((nslab, R * S5_PITCH, LANES), F32),
            pltpu.VMEM((2, 2, R, sw), F32),
            pltpu.VMEM((2, wide_rows, S5_PIECE), F32),
            pltpu.VMEM((2, wide_rows, ssm_w // 2), BF16),
        ],
        compiler_params=_cparams(("arbitrary",)),
        name="s5_scan",
    )(su, su, su, bm, cm, a_re, a_im)


def _mix_out_kernel(*refs, n_x, n_ctx_tiles, off, chains):
    x_refs = refs[:n_x]
    oa_ref = refs[n_x]
    s5_refs = refs[n_x + 1:n_x + 1 + 3 * chains]
    (gm_ref, d_ref, wg_ref, bg_ref, lg_ref, lb_ref, ws_ref, bs_ref, wo_ref, gate_ref, gf_ref, shf_ref, scf_ref,
     o_ref, h_ref) = refs[n_x + 1 + 3 * chains:]
    sub = o_ref.shape[0] // chains
    gw = gm_ref.shape[1] // 2
    bs = bs_ref[...]
    for ch in range(chains):
        rows = slice(ch * sub, (ch + 1) * sub)
        yf_ref, yb_ref, su_ref = s5_refs[3 * ch:3 * ch + 3]
        y = d_ref[...] * su_ref[...] + yf_ref[...] + yb_ref[...]
        g1 = _gelu(y)
        o_ssm = g1 * _sigmoid(_dot(g1.astype(BF16), wg_ref[...]) + bg_ref[...])
        u = _gelu(gm_ref[rows, :gw])
        v = _gelu(gm_ref[rows, gw:])
        mu = jnp.mean(v, axis=-1, keepdims=True)
        var = jnp.mean(jnp.square(v - mu), axis=-1, keepdims=True)
        v = ((v - mu) * lax.rsqrt(var + NORM_EPS) * lg_ref[...] + lb_ref[...]).astype(BF16)
        mixed = []
        for c in range(sub // GMLP_CHUNK):
            cols = []
            for g in range(gw // GMLP_GROUP_W):
                vb = v[c * GMLP_CHUNK:(c + 1) * GMLP_CHUNK, g * GMLP_GROUP_W:(g + 1) * GMLP_GROUP_W]
                cols.append(_dot(ws_ref[g], vb) + bs[:, g:g + 1])
            mixed.append(jnp.concatenate(cols, axis=1))
        o_gmlp = u * jnp.concatenate(mixed, axis=0)
        mix = jnp.concatenate([oa_ref[rows, :], o_ssm.astype(BF16), o_gmlp.astype(BF16)], axis=1)
        x1 = _read_x(x_refs, n_ctx_tiles, off, rows) + gate_ref[...] * _dot(mix, wo_ref[...])
        o_ref[rows, :] = x1
        h_ref[rows, :] = _norm_mod(x1, gf_ref[...], shf_ref[...], scf_ref[...]).astype(BF16)


def _mix_out(layer, xs, o_attn, y_f, y_b, su, gm, p, g_ffn, mod4, dims, skip_ctx):
    B, n_c, L, ntok = dims["B"], dims["n_c"], dims["L"], dims["ntok"]
    d = xs[0].shape[1]
    sub = n_c
    lt = L // sub
    chains = 2 if (B % 2 == 0 and lt % 2 == 0) else 1
    tm = chains * sub
    attn_w, ssm_w, gm_w = o_attn.shape[1], su.shape[2], gm.shape[1]
    ngrp = p["w_s"].shape[1]
    nct = B * n_c // tm
    off = nct if skip_ctx else 0
    nrow = ntok - off * tm
    assert o_attn.shape[0] == nrow

    def row(i):
        return jnp.where(i + off < nct, B, (i + off - nct) // (L // tm))

    def seq(ch):
        def index(i):
            j = (i + off) * chains + ch
            return (jnp.where(j < B, j, (j - B) // lt), jnp.where(j < B, 0, 1 + (j - B) % lt), 0)
        return index

    vec = lambda w: pl.BlockSpec((None, 1, w), lambda i: (layer, 0, 0))
    modspec = lambda k: pl.BlockSpec((None, None, 1, d), lambda i: (layer, row(i), 0, k))
    s5_specs = [pl.BlockSpec((None, sub, ssm_w), seq(ch)) for ch in range(chains) for _ in range(3)]
    s5_args = [a for _ in range(chains) for a in (y_f, y_b, su)]
    kern = functools.partial(_mix_out_kernel, n_x=len(xs), n_ctx_tiles=nct, off=off, chains=chains)
    return pl.pallas_call(
        kern,
        grid=(nrow // tm,),
        in_specs=_x_specs(xs, tm, nct, off) + [pl.BlockSpec((tm, attn_w), lambda i: (i, 0))] + s5_specs + [
            pl.BlockSpec((tm, gm_w), lambda i: (i + off, 0)),
            vec(ssm_w),
            _resident((None, ssm_w, ssm_w), lambda i: (layer, 0, 0)),
            vec(ssm_w), vec(gm_w // 2), vec(gm_w // 2),
            _resident((None, ngrp, GMLP_CHUNK, GMLP_CHUNK), lambda i: (layer, 0, 0, 0)),
            pl.BlockSpec((None, GMLP_CHUNK, ngrp), lambda i: (layer, 0, 0)),
            _resident((None, d, d), lambda i: (layer, 0, 0)),
            modspec(2), vec(d), modspec(3), modspec(4),
        ],
        out_specs=[pl.BlockSpec((tm, d), lambda i: (i, 0)), pl.BlockSpec((tm, d), lambda i: (i, 0))],
        out_shape=[jax.ShapeDtypeStruct((nrow, d), F32), jax.ShapeDtypeStruct((nrow, d), BF16)],
        compiler_params=_cparams(("arbitrary",)),
        name="mix_out",
    )(*xs, o_attn, *s5_args, gm, p["ssm_d"], p["w_glu"], p["b_glu"], p["ln_g"], p["ln_b"],
      p["w_s"], p["b_s_t"], p["w_out"], mod4, g_ffn, mod4, mod4)


def _ffn_kernel(hm_ref, hp_ref, hn_ref, x_ref, gate_ref, gfin_ref, wg_ref, wv_ref, cw_ref, cb_ref, wd_ref,
                o_ref, h_ref, mp_ref, mn_ref, *, n_ctx_tiles, n_c, L, final):
    i = pl.program_id(0)
    j = pl.program_id(1)
    tm = x_ref.shape[0]
    halo = hp_ref.shape[0]

    @pl.when(j == 0)
    def _():
        h_ref[0:halo, :] = hp_ref[...]
        h_ref[halo:halo + tm, :] = hm_ref[...]
        h_ref[halo + tm:, :] = hn_ref[...]
        is_ctx = i < n_ctx_tiles
        seq_len = jnp.where(is_ctx, n_c, L)
        start = lax.rem(jnp.where(is_ctx, i * tm, (i - n_ctx_tiles) * tm), seq_len)
        pos = start + lax.broadcasted_iota(jnp.int32, (tm, LANES), 0)
        first = pos == 0
        last = pos == seq_len - 1
        for k in range(1, tm // min(n_c, L) + 1):
            first = jnp.logical_or(first, pos == k * seq_len)
            last = jnp.logical_or(last, pos == (k + 1) * seq_len - 1)
        mp_ref[...] = jnp.where(first, 0.0, 1.0)
        mn_ref[...] = jnp.where(last, 0.0, 1.0)
        o_ref[...] = jnp.zeros_like(o_ref)

    tf = wg_ref.shape[1]
    rep = tf // LANES
    ge = _dot(h_ref[...], wg_ref[...])
    val = _dot(h_ref[halo:halo + tm, :], wv_ref[...])
    ext = tm + 2 * halo
    g_prev = pltpu.roll(ge, 1, axis=0)[halo:halo + tm]
    g_next = pltpu.roll(ge, ext - 1, axis=0)[halo:halo + tm]
    mp = jnp.concatenate([mp_ref[...]] * rep, axis=1)
    mn = jnp.concatenate([mn_ref[...]] * rep, axis=1)
    cw = cw_ref[...]
    gc = cw[0:1] * (g_prev * mp) + cw[1:2] * ge[halo:halo + tm] + cw[2:3] * (g_next * mn) + cb_ref[...]
    act = (gc * _sigmoid(gc) * val).astype(BF16)
    o_ref[...] += _dot(act, wd_ref[...])

    @pl.when(j == pl.num_programs(1) - 1)
    def _():
        xn = x_ref[...] + gate_ref[...] * o_ref[...]
        if final:
            xn = xn * lax.rsqrt(jnp.mean(xn * xn, axis=-1, keepdims=True) + NORM_EPS) * gfin_ref[...]
        o_ref[...] = xn


def _conv_ffn(layer, x1, h2, mod4, w_up, conv_w, conv_b, w_down, g_final, dims, skip_ctx, final):
    B, n_c, L = dims["B"], dims["n_c"], dims["L"]
    nrow, d = x1.shape
    d_ff = w_down.shape[1]
    tm, tf = dims["ffn_tm"], FFN_TF
    halo = BF16_ROWS
    nf = d_ff // tf
    nct = 0 if skip_ctx else B * n_c // tm
    lt = L // tm
    nhalo = nrow // halo

    def row(i):
        return jnp.where(i < nct, B, (i - nct) // lt)

    modspec = lambda k: pl.BlockSpec((None, None, 1, d), lambda i, j: (layer, row(i), 0, k))
    kern = functools.partial(_ffn_kernel, n_ctx_tiles=nct, n_c=n_c, L=L, final=final)
    return pl.pallas_call(
        kern,
        grid=(nrow // tm, nf),
        in_specs=[
            _resident((tm, d), lambda i, j: (i, 0)),
            pl.BlockSpec((halo, d), lambda i, j: (jnp.maximum(i * (tm // halo) - 1, 0), 0)),
            pl.BlockSpec((halo, d), lambda i, j: (jnp.minimum((i + 1) * (tm // halo), nhalo - 1), 0)),
            _resident((tm, d), lambda i, j: (i, 0)),
            modspec(5),
            pl.BlockSpec((1, d), lambda i, j: (0, 0)),
            pl.BlockSpec((None, d, tf), lambda i, j: (layer, 0, j)),
            pl.BlockSpec((None, d, tf), lambda i, j: (layer, 0, nf + j)),
            pl.BlockSpec((None, 3, tf), lambda i, j: (layer, 0, j)),
            pl.BlockSpec((None, 1, tf), lambda i, j: (layer, 0, j)),
            pl.BlockSpec((None, tf, d), lambda i, j: (layer, j, 0)),
        ],
        out_specs=pl.BlockSpec((tm, d), lambda i, j: (i, 0)),
        out_shape=jax.ShapeDtypeStruct(x1.shape, F32),
        scratch_shapes=[
            pltpu.VMEM((tm + 2 * halo, d), BF16),
            pltpu.VMEM((tm, LANES), F32),
            pltpu.VMEM((tm, LANES), F32),
        ],
        compiler_params=_cparams(("arbitrary", "arbitrary")),
        name="conv_ffn",
    )(h2, h2, h2, x1, mod4, g_final, w_up, w_up, conv_w, conv_b, w_down)


def _rope_tables(n_c, L):
    n_freq = HEAD_DIM // 4
    t = np.arange(L)
    inv_freq = jnp.asarray(ROPE_BASE, F32) ** (-jnp.arange(n_freq, dtype=F32) / n_freq)
    pos = jnp.asarray(np.stack([t // GRID_W, t % GRID_W], axis=-1), F32)
    ang = pos[:, :, None] * inv_freq
    cos, sin = jnp.cos(ang), jnp.sin(ang)
    c64 = jnp.concatenate([cos, cos], axis=-1).reshape(L, HEAD_DIM)
    s64 = jnp.concatenate([-sin, sin], axis=-1).reshape(L, HEAD_DIM)
    reps = LANES // HEAD_DIM
    c = jnp.concatenate([jnp.ones((n_c, LANES), F32), jnp.tile(c64, (1, reps))], axis=0)
    s = jnp.concatenate([jnp.zeros((n_c, LANES), F32), jnp.tile(s64, (1, reps))], axis=0)
    return c, s


def kernel(x, c, ctx, c_ctx, w_ada, b_ada, g_mix, g_ffn, w_in, w_out, attn_sink, ssm_lambda_re, ssm_lambda_im,
           ssm_log_dt, ssm_b_re, ssm_b_im, ssm_c_re, ssm_c_im, ssm_d, ssm_w_glu, ssm_b_glu, gmlp_ln_g, gmlp_ln_b,
           gmlp_w_s, gmlp_b_s, ffn_w_up, ffn_conv_w, ffn_conv_b, ffn_w_down, g_final):
    B, L, d = x.shape
    n_c = ctx.shape[1]
    depth = w_in.shape[0]
    ssm_w = ssm_d.shape[1]
    gmlp_w = gmlp_ln_g.shape[1]
    attn_w = w_in.shape[2] - ssm_w - 2 * gmlp_w
    attn_w = attn_w * GQA_RATIO // (GQA_RATIO + 2)
    kv_w = attn_w // GQA_RATIO
    ntok = B * (n_c + L)
    ffn_tm = min(FFN_TM, B * n_c)
    assert n_c % ATTN_BLOCK == 0 and L % n_c == 0 and L % ffn_tm == 0 and (B * n_c) % ffn_tm == 0
    assert n_c % S5_T == 0 and L % S5_T == 0 and B + 1 <= SUBLANES
    dims = dict(B=B, n_c=n_c, L=L, ntok=ntok, attn_w=attn_w, kv_w=kv_w, ssm_w=ssm_w, ffn_tm=ffn_tm)

    cpad = jnp.zeros((SUBLANES, d), F32).at[:B].set(c).at[B].set(c_ctx)
    mod = _modulation(cpad, w_ada, b_ada)
    mod4 = mod.reshape(depth, SUBLANES, 1, N_MOD * d)

    bm, cm, a_re, a_im = _s5_prepare(ssm_lambda_re, ssm_lambda_im, ssm_log_dt, ssm_b_re, ssm_b_im,
                                     ssm_c_re, ssm_c_im, B)
    cos_t, sin_t = _rope_tables(n_c, L)
    bias = _attn_bias()

    w_in_b = w_in.astype(BF16)
    w_up_b = ffn_w_up.astype(BF16)
    w_down_b = ffn_w_down.astype(BF16)
    mixp = dict(
        ssm_d=ssm_d.reshape(depth, 1, ssm_w), w_glu=ssm_w_glu.astype(BF16), b_glu=ssm_b_glu.reshape(depth, 1, ssm_w),
        ln_g=gmlp_ln_g.reshape(depth, 1, gmlp_w), ln_b=gmlp_ln_b.reshape(depth, 1, gmlp_w),
        w_s=gmlp_w_s.astype(BF16), b_s_t=jnp.swapaxes(gmlp_b_s, 1, 2), w_out=w_out.astype(BF16))
    g_mix3 = g_mix.reshape(depth, 1, d)
    g_ffn3 = g_ffn.reshape(depth, 1, d)
    conv_b3 = ffn_conv_b.reshape(depth, 1, -1)

    xs = (ctx.reshape(B * n_c, d), x.reshape(B * L, d))
    g_fin = g_final.reshape(1, d)
    for layer in range(depth):
        last = layer == depth - 1
        q, kx, vx, su, gm = _in_proj(layer, xs, g_mix3, mod4, w_in_b, cos_t, sin_t, dims)
        o_attn = _attention(layer, q, kx, vx, attn_sink, bias, dims, skip_ctx=last)
        y_f, y_b = _s5_scan(layer, su, bm, cm, a_re, a_im, dims)
        x1, h2 = _mix_out(layer, xs, o_attn, y_f, y_b, su, gm, mixp, g_ffn3, mod4, dims, skip_ctx=last)
        xs = (_conv_ffn(layer, x1, h2, mod4, w_up_b, ffn_conv_w, conv_b3, w_down_b, g_fin, dims,
                        skip_ctx=last, final=last),)
    return xs[0].reshape(B, L, d)
```
